```python
import math
import jax, jax.numpy as jnp
from jax import lax
import numpy as np

D_MODEL = 1024
BATCH = 4
SEQ = 4096
DEPTH = 1
DEC_BATCH = 32
DEC_SEQ = 1
PAST_LEN = 16384
PAGE_SIZE = 128

HEAD_DIM = 64
ATTN_WIDTH = D_MODEL // 2
N_HEADS = ATTN_WIDTH // HEAD_DIM
CONV_DIM = D_MODEL - ATTN_WIDTH
CONV_W = 3
MOBA_BLOCK = 256
MOBA_TOPK = 3
Q_CHUNK = 64
D_FF = ((8 * D_MODEL + 3 * 256 - 1) // (3 * 256)) * 256
MIX_COLS = 3 * ATTN_WIDTH + 3 * CONV_DIM
RMS_EPS = 1e-6
ATTN_SCALE = HEAD_DIM ** -0.5

kernel_name = 'hymba_moba_shortconv_decoder_step'


def _rmsnorm(x, g):
    xf = x.astype(jnp.float32)
    y = xf * lax.rsqrt(jnp.mean(xf * xf, axis=-1, keepdims=True) + RMS_EPS)
    return (y * g.astype(jnp.float32)).astype(x.dtype)


def _alibi_slopes():
    return 2.0 ** (-8.0 * jnp.arange(1, N_HEADS + 1, dtype=jnp.float32) / N_HEADS)


def _split_mix(h, w_mix):
    z = jnp.einsum('bsd,de->bse', h, w_mix)
    a, c = ATTN_WIDTH, CONV_DIM
    q, k, v, gb, gc, u = jnp.split(z, [a, 2 * a, 3 * a, 3 * a + c, 3 * a + 2 * c], axis=-1)
    hs = lambda t: t.reshape(t.shape[:2] + (N_HEADS, HEAD_DIM))
    return hs(q), hs(k), hs(v), gb, gc, u


def _num_blocks(length):
    return max(-(-length // MOBA_BLOCK), MOBA_TOPK)


def _block_means(page_sums, nb):
    ppb = MOBA_BLOCK // PAGE_SIZE
    ps = jnp.pad(page_sums, ((0, 0), (0, nb * ppb - page_sums.shape[1]), (0, 0), (0, 0)))
    return ps.reshape(ps.shape[0], nb, ppb, N_HEADS, HEAD_DIM).sum(axis=2) / MOBA_BLOCK


def _moba_chunk(qc, pos_c, kmean, fetch, slopes):
    B, Qc = qc.shape[0], qc.shape[1]
    nb = kmean.shape[1]
    own = pos_c // MOBA_BLOCK
    gate = jnp.einsum('bqhd,bnhd->bqhn', qc.astype(jnp.float32), kmean.astype(jnp.float32))
    past_ok = jnp.arange(nb, dtype=jnp.int32)[None, :] < own[:, None]
    gate = jnp.where(past_ok[None, :, None, :], gate, -jnp.inf)
    _, top = lax.top_k(gate, MOBA_TOPK)
    own_b = jnp.broadcast_to(own[None, :, None, None], (B, Qc, N_HEADS, 1))
    blk = jnp.concatenate([top, own_b], axis=-1)
    blk_ok = jnp.concatenate([top < own_b, jnp.ones_like(own_b, dtype=bool)], axis=-1)
    kpos = blk[..., None] * MOBA_BLOCK + jnp.arange(MOBA_BLOCK, dtype=jnp.int32)
    kg, vg = fetch(kpos)
    dist = pos_c[None, :, None, None, None] - kpos
    logits = (jnp.einsum('bqhd,bqhnkd->bqhnk', qc, kg).astype(jnp.float32) * ATTN_SCALE
              - slopes[None, None, :, None, None] * dist.astype(jnp.float32))
    logits = jnp.where(blk_ok[..., None] & (dist >= 0), logits, -jnp.inf)
    p = jax.nn.softmax(logits.reshape(B, Qc, N_HEADS, -1), axis=-1).reshape(logits.shape)
    return jnp.einsum('bqhnk,bqhnkd->bqhd', p.astype(vg.dtype), vg)


def _moba_prompt(q, k, v, slopes):
    B, S = q.shape[0], q.shape[1]
    nb = _num_blocks(S)
    page_sums = k.astype(jnp.float32).reshape(B, S // PAGE_SIZE, PAGE_SIZE, N_HEADS, HEAD_DIM).sum(axis=2)
    kmean = _block_means(page_sums, nb)
    bi = jnp.arange(B)[:, None, None, None, None]
    hi = jnp.arange(N_HEADS)[None, None, :, None, None]

    def fetch(kpos):
        s = jnp.clip(kpos, 0, S - 1)
        return k[bi, s, hi], v[bi, s, hi]

    n_chunks = S // Q_CHUNK
    qs = q.reshape(B, n_chunks, Q_CHUNK, N_HEADS, HEAD_DIM).transpose(1, 0, 2, 3, 4)
    ps = jnp.arange(S, dtype=jnp.int32).reshape(n_chunks, Q_CHUNK)
    out = lax.map(lambda a: _moba_chunk(a[0], a[1], kmean, fetch, slopes), (qs, ps))
    return out.transpose(1, 0, 2, 3, 4).reshape(B, S, ATTN_WIDTH)


def _moba_sample(q, k_new, v_new, cache_k, cache_v, layer, page_table, slopes):
    B, T = q.shape[0], q.shape[1]
    n_pages = page_table.shape[1]
    nb = _num_blocks(PAST_LEN + T)
    page_sums = cache_k[layer, page_table].astype(jnp.float32).sum(axis=2)
    new_pos = PAST_LEN + jnp.arange(T, dtype=jnp.int32)
    new_sums = jax.ops.segment_sum(k_new.astype(jnp.float32).transpose(1, 0, 2, 3),
                                   new_pos // MOBA_BLOCK, num_segments=nb)
    kmean = _block_means(page_sums, nb) + new_sums.transpose(1, 0, 2, 3) / MOBA_BLOCK
    bi = jnp.arange(B)[:, None, None, None, None]
    hi = jnp.arange(N_HEADS)[None, None, :, None, None]

    def fetch(kpos):
        in_past = (kpos < PAST_LEN)[..., None]
        pg = page_table[bi, jnp.clip(kpos // PAGE_SIZE, 0, n_pages - 1)]
        off = kpos % PAGE_SIZE
        j = jnp.clip(kpos - PAST_LEN, 0, T - 1)
        kg = jnp.where(in_past, cache_k[layer, pg, off, hi].astype(k_new.dtype), k_new[bi, j, hi])
        vg = jnp.where(in_past, cache_v[layer, pg, off, hi].astype(v_new.dtype), v_new[bi, j, hi])
        return kg, vg

    out = _moba_chunk(q, new_pos, kmean, fetch, slopes)
    return out.reshape(B, T, ATTN_WIDTH)


def _short_conv(gb, gc, u, hist, w_conv):
    xc = gc * u
    full = jnp.concatenate([hist.astype(xc.dtype), xc], axis=1)
    S = xc.shape[1]
    y = w_conv[0] * full[:, 0:S]
    for j in range(1, CONV_W):
        y = y + w_conv[j] * full[:, j:j + S]
    return gb * y, full[:, -(CONV_W - 1):]


def _block(x, attn_fn, conv_hist, norm_mix, w_mix, w_conv, norm_attn_out, norm_conv_out,
           w_o, norm_ffn, w_gate, w_up, w_down):
    h = _rmsnorm(x, norm_mix)
    q, k, v, gb, gc, u = _split_mix(h, w_mix)
    a = attn_fn(q, k, v)
    c, conv_state = _short_conv(gb, gc, u, conv_hist, w_conv)
    mixed = jnp.concatenate([_rmsnorm(a, norm_attn_out), _rmsnorm(c, norm_conv_out)], axis=-1)
    x = x + mixed @ w_o
    h2 = _rmsnorm(x, norm_ffn)
    x = x + (jax.nn.silu(h2 @ w_gate) * (h2 @ w_up)) @ w_down
    return x, k, v, conv_state


def setup_inputs(seed: int = 0) -> dict:
    key = jax.random.key(seed)
    ks = jax.random.split(key, 20)
    n_pages = PAST_LEN // PAGE_SIZE
    n_pool = (5 * DEC_BATCH * n_pages + 3) // 4
    f32 = jnp.float32

    def nrm(k, shape, scale):
        return jax.random.normal(k, shape, f32) * scale

    def gain(k, n):
        return 1.0 + 0.02 * jax.random.normal(k, (DEPTH, n), f32)

    page_table = jax.random.permutation(ks[5], n_pool)[: DEC_BATCH * n_pages].reshape(
        DEC_BATCH, n_pages).astype(jnp.int32)
    return {
        'x_prompt': nrm(ks[0], (BATCH, SEQ, D_MODEL), 1.0),
        'x_sample': nrm(ks[1], (DEC_BATCH, DEC_SEQ, D_MODEL), 1.0),
        'cache_k': nrm(ks[2], (DEPTH, n_pool, PAGE_SIZE, N_HEADS, HEAD_DIM), 1.0),
        'cache_v': nrm(ks[3], (DEPTH, n_pool, PAGE_SIZE, N_HEADS, HEAD_DIM), 1.0),
        'state_conv': nrm(ks[4], (DEPTH, DEC_BATCH, CONV_W - 1, CONV_DIM), 1.0),
        'page_table': page_table,
        'norm_mix': gain(ks[6], D_MODEL),
        'w_mix': nrm(ks[7], (DEPTH, D_MODEL, MIX_COLS), D_MODEL ** -0.5),
        'w_conv': nrm(ks[8], (DEPTH, CONV_W, CONV_DIM), CONV_W ** -0.5),
        'norm_attn_out': gain(ks[9], ATTN_WIDTH),
        'norm_conv_out': gain(ks[10], CONV_DIM),
        'w_o': nrm(ks[11], (DEPTH, ATTN_WIDTH + CONV_DIM, D_MODEL), D_MODEL ** -0.5),
        'norm_ffn': gain(ks[12], D_MODEL),
        'w_gate': nrm(ks[13], (DEPTH, D_MODEL, D_FF), D_MODEL ** -0.5),
        'w_up': nrm(ks[14], (DEPTH, D_MODEL, D_FF), D_MODEL ** -0.5),
        'w_down': nrm(ks[15], (DEPTH, D_FF, D_MODEL), D_FF ** -0.5),
        'norm_final': 1.0 + 0.02 * jax.random.normal(ks[16], (D_MODEL,), f32),
    }


def reference(x_prompt, x_sample, cache_k, cache_v, state_conv, page_table, norm_mix, w_mix,
              w_conv, norm_attn_out, norm_conv_out, w_o, norm_ffn, w_gate, w_up, w_down,
              norm_final):
    slopes = _alibi_slopes()
    yp, ys = x_prompt, x_sample
    kp, vp, cp, ks_, vs_, cs_ = [], [], [], [], [], []
    for l in range(DEPTH):
        w = (norm_mix[l], w_mix[l], w_conv[l], norm_attn_out[l], norm_conv_out[l], w_o[l],
             norm_ffn[l], w_gate[l], w_up[l], w_down[l])
        hist0 = jnp.zeros((yp.shape[0], CONV_W - 1, CONV_DIM), yp.dtype)
        yp, k1, v1, c1 = _block(yp, lambda q, k, v: _moba_prompt(q, k, v, slopes), hist0, *w)
        ys, k2, v2, c2 = _block(
            ys, lambda q, k, v: _moba_sample(q, k, v, cache_k, cache_v, l, page_table, slopes),
            state_conv[l], *w)
        kp.append(k1); vp.append(v1); cp.append(c1)
        ks_.append(k2); vs_.append(v2); cs_.append(c2)
    yp = _rmsnorm(yp, norm_final)
    ys = _rmsnorm(ys, norm_final)
    return (yp, ys, jnp.stack(kp), jnp.stack(vp), jnp.stack(cp), jnp.stack(ks_), jnp.stack(vs_), jnp.stack(cs_))
```

```python
import functools

import numpy as np
import jax
import jax.numpy as jnp
from jax import lax
from jax.experimental import pallas as pl
from jax.experimental.pallas import tpu as pltpu

F32 = jnp.float32
BF16 = jnp.bfloat16

HEAD_DIM = 64
N_HEADS = 8
ATTN_WIDTH = N_HEADS * HEAD_DIM
MOBA_BLOCK = 256
MOBA_TOPK = 3
PAGE_SIZE = 128
PAGES_PER_BLOCK = MOBA_BLOCK // PAGE_SIZE
RMS_EPS = 1e-6
ATTN_SCALE = HEAD_DIM ** -0.5
NEG = -1e30
GATE_LANES = 16
LANE = 128
SUBLANE = 8
VMEM_LIMIT = 56 * 1024 * 1024


def _alibi_slopes_np():
    return (2.0 ** (-8.0 * np.arange(1, N_HEADS + 1, dtype=np.float64) / N_HEADS)).astype(np.float32)


def _rms(x, g):
    y = x * lax.rsqrt(jnp.mean(x * x, axis=-1, keepdims=True) + RMS_EPS)
    return y * g


def _split_bf16(x):
    hi = x.astype(BF16)
    lo = (x - hi.astype(F32)).astype(BF16)
    return hi, lo


_DN_T = (((1,), (1,)), ((), ()))


def _mix_prompt_kernel(x_ref, g_ref, w_ref, wc_ref, gco_ref,
                       q_ref, k_ref, v_ref, cn_ref, selb_ref, cst_ref,
                       carry_ref, kmean_ref):
    s = pl.program_id(1)
    tm = x_ref.shape[0]
    a = ATTN_WIDTH
    c = cn_ref.shape[1]

    @pl.when(s == 0)
    def _():
        carry_ref[...] = jnp.zeros_like(carry_ref)
        kmean_ref[...] = jnp.zeros_like(kmean_ref)

    h = _rms(x_ref[...], g_ref[...]).astype(BF16)
    z = jnp.dot(h, w_ref[...], preferred_element_type=F32)
    q = z[:, 0:a]
    k = z[:, a:2 * a]
    v = z[:, 2 * a:3 * a]
    gb = z[:, 3 * a:3 * a + c]
    gc = z[:, 3 * a + c:3 * a + 2 * c]
    u = z[:, 3 * a + 2 * c:3 * a + 3 * c]

    q_ref[...] = (q * ATTN_SCALE).astype(BF16)
    k_ref[...] = k
    v_ref[...] = v

    xc = gc * u
    row = lax.broadcasted_iota(jnp.int32, xc.shape, 0)
    p0 = carry_ref[0:1, :]
    p1 = carry_ref[1:2, :]
    xc1 = jnp.where(row == 0, p1, pltpu.roll(xc, 1, axis=0))
    xc2 = jnp.where(row == 0, p0, jnp.where(row == 1, p1, pltpu.roll(xc, 2, axis=0)))
    y = wc_ref[0:1, :] * xc2 + wc_ref[1:2, :] * xc1 + wc_ref[2:3, :] * xc
    cn_ref[...] = _rms(gb * y, gco_ref[...]).astype(BF16)
    last2 = xc[tm - 2:tm, :]
    carry_ref[...] = last2
    cst_ref[...] = last2

    nb = kmean_ref.shape[0]
    km = kmean_ref[...]
    kmt = jnp.concatenate([km] * N_HEADS, axis=0)
    rr = lax.broadcasted_iota(jnp.int32, kmt.shape, 0) // nb
    cc = lax.broadcasted_iota(jnp.int32, kmt.shape, 1) // HEAD_DIM
    kmt = jnp.where(rr == cc, kmt, 0.0)
    qh, ql = _split_bf16(q)
    kh, kl = _split_bf16(kmt)
    gate = (lax.dot_general(qh, kh, _DN_T, preferred_element_type=F32)
            + lax.dot_general(ql, kh, _DN_T, preferred_element_type=F32)
            + lax.dot_general(qh, kl, _DN_T, preferred_element_type=F32))

    lane = lax.broadcasted_iota(jnp.int32, gate.shape, 1)
    jl = lane % nb
    valid = jl < s
    gm = jnp.where(valid, gate, -jnp.inf)
    cnt = jnp.zeros(gate.shape, F32)
    width = gate.shape[1]
    for sh in range(1, nb):
        fwd = pltpu.roll(gm, width - sh, axis=1)
        bwd = pltpu.roll(gm, nb - sh, axis=1)
        wrapped = (jl + sh) >= nb
        beats = jnp.where(wrapped, jnp.where(bwd >= gm, 1.0, 0.0), jnp.where(fwd > gm, 1.0, 0.0))
        cnt = cnt + beats
    selb = jnp.where(valid, jnp.where(cnt < MOBA_TOPK, 0.0, NEG), NEG)
    selb = jnp.where(jl == s, 0.0, selb)
    for hp in range(N_HEADS // 2):
        part = selb if hp == 0 else pltpu.roll(selb, width - 2 * nb * hp, axis=1)
        selb_ref[:, hp * LANE:(hp + 1) * LANE] = jnp.where(lane < 2 * nb, part, 0.0).astype(BF16)

    kmean_ref[pl.ds(s, 1), :] = jnp.sum(k, axis=0, keepdims=True) * (1.0 / MOBA_BLOCK)


def _mix_prompt(x, g, w_bf, wc, gco):
    b, s, d = x.shape
    tm = MOBA_BLOCK
    ns = s // tm
    a = ATTN_WIDTH
    c = wc.shape[1]
    assert s % tm == 0 and ns * N_HEADS == LANE and 2 * ns <= LANE and w_bf.shape[1] == 3 * a + 3 * c
    row_spec = lambda n: pl.BlockSpec((None, tm, n), lambda i, j: (i, j, 0))
    full = lambda shp: pl.BlockSpec(shp, lambda i, j: (0,) * len(shp))
    return pl.pallas_call(
        _mix_prompt_kernel,
        grid=(b, ns),
        in_specs=[row_spec(d), full((1, d)), full(w_bf.shape), full(wc.shape), full((1, c))],
        out_specs=[row_spec(a), row_spec(a), row_spec(a), row_spec(c), row_spec(a),
                   pl.BlockSpec((None, 2, c), lambda i, j: (i, 0, 0))],
        out_shape=[jax.ShapeDtypeStruct((b, s, a), BF16),
                   jax.ShapeDtypeStruct((b, s, a), F32),
                   jax.ShapeDtypeStruct((b, s, a), F32),
                   jax.ShapeDtypeStruct((b, s, c), BF16),
                   jax.ShapeDtypeStruct((b, s, a), BF16),
                   jax.ShapeDtypeStruct((b, 2, c), F32)],
        scratch_shapes=[pltpu.VMEM((2, c), F32), pltpu.VMEM((ns, a), F32)],
        compiler_params=pltpu.CompilerParams(dimension_semantics=("arbitrary", "arbitrary"),
                                             vmem_limit_bytes=VMEM_LIMIT),
        name="mix_prompt",
    )(x, g, w_bf, wc, gco)


def _attn_prompt_kernel(slope_ref, q_ref, k_ref, v_ref, selb_ref, o_ref, kb_ref, vt_ref):
    hp = pl.program_id(1)
    i = pl.program_id(2)
    blk = MOBA_BLOCK
    nb = kb_ref.shape[0]

    @pl.when(i == 0)
    def _():
        for cidx in range(nb):
            rows = slice(cidx * blk, (cidx + 1) * blk)
            kb_ref[cidx] = k_ref[rows, :].astype(BF16)
            vt_ref[cidx] = jnp.transpose(v_ref[rows, :]).astype(BF16)

    q2 = q_ref[...].astype(F32)
    sb = selb_ref[...].astype(F32)
    lane = lax.broadcasted_iota(jnp.int32, q2.shape, 1)
    ws = []
    for hh in range(2):
        head = 2 * hp + hh
        qm = jnp.where((lane >= HEAD_DIM * hh) & (lane < HEAD_DIM * (hh + 1)), q2, 0.0)
        aug = jnp.where((lane >= GATE_LANES * hh) & (lane < GATE_LANES * (hh + 1)), sb, 0.0)
        for part in range(3):
            sl = slope_ref[part * N_HEADS + head]
            aug = jnp.where((lane == 32 + part) | (lane == 35 + part), sl, aug)
        lhs = jnp.concatenate([qm, aug], axis=1)
        ws.append(jnp.transpose(lhs).astype(BF16))

    elane = lax.broadcasted_iota(jnp.int32, (blk, LANE), 1)
    erow = lax.broadcasted_iota(jnp.int32, (blk, LANE), 0).astype(F32)
    ebase = jnp.where((elane >= 32) & (elane < 35), erow, 0.0)

    def scores(j):
        off = ((j - i) * blk).astype(F32)
        e = jnp.where((elane == j) | (elane == GATE_LANES + j), 1.0,
                      jnp.where((elane >= 35) & (elane < 38), off, ebase))
        rhs = jnp.concatenate([kb_ref[j], e.astype(BF16)], axis=1)
        return [jnp.dot(rhs, w, preferred_element_type=F32) for w in ws]

    rk = lax.broadcasted_iota(jnp.int32, (blk, blk), 0)
    rq = lax.broadcasted_iota(jnp.int32, (blk, blk), 1)
    tri = jnp.where(rk <= rq, 0.0, NEG)
    state = []
    vt_i = vt_ref[i]
    for hh, st in enumerate(scores(i)):
        st = st + tri
        m = jnp.max(st, axis=0, keepdims=True)
        p = jnp.exp(st - m)
        l = jnp.sum(p, axis=0, keepdims=True)
        acc = jnp.dot(vt_i[HEAD_DIM * hh:HEAD_DIM * (hh + 1), :], p.astype(BF16),
                      preferred_element_type=F32)
        state += [m, l, acc]

    def body(j, carry):
        out = []
        vt_j = vt_ref[j]
        for hh, st in enumerate(scores(j)):
            m, l, acc = carry[3 * hh:3 * hh + 3]
            m_new = jnp.maximum(m, jnp.max(st, axis=0, keepdims=True))
            alpha = jnp.exp(m - m_new)
            p = jnp.exp(st - m_new)
            l = alpha * l + jnp.sum(p, axis=0, keepdims=True)
            acc = alpha * acc + jnp.dot(vt_j[HEAD_DIM * hh:HEAD_DIM * (hh + 1), :], p.astype(BF16),
                                        preferred_element_type=F32)
            out += [m_new, l, acc]
        return tuple(out)

    state = lax.fori_loop(0, i, body, tuple(state))
    o_t = jnp.concatenate([state[2] / state[1], state[5] / state[4]], axis=0)
    o_ref[...] = jnp.transpose(o_t)


def _attn_prompt(q, k, v, selb, slope_parts):
    b, s, a = k.shape
    blk = MOBA_BLOCK
    nb = s // blk
    assert nb <= GATE_LANES and a == ATTN_WIDTH
    grid_spec = pltpu.PrefetchScalarGridSpec(
        num_scalar_prefetch=1,
        grid=(b, N_HEADS // 2, nb),
        in_specs=[pl.BlockSpec((None, blk, LANE), lambda bi, hp, i, sl: (bi, i, hp)),
                  pl.BlockSpec((None, s, LANE), lambda bi, hp, i, sl: (bi, 0, hp)),
                  pl.BlockSpec((None, s, LANE), lambda bi, hp, i, sl: (bi, 0, hp)),
                  pl.BlockSpec((None, blk, LANE), lambda bi, hp, i, sl: (bi, i, hp))],
        out_specs=pl.BlockSpec((None, blk, LANE), lambda bi, hp, i, sl: (bi, i, hp)),
        scratch_shapes=[pltpu.VMEM((nb, blk, LANE), BF16), pltpu.VMEM((nb, LANE, blk), BF16)])
    return pl.pallas_call(
        _attn_prompt_kernel,
        grid_spec=grid_spec,
        out_shape=jax.ShapeDtypeStruct((b, s, a), F32),
        compiler_params=pltpu.CompilerParams(
            dimension_semantics=("arbitrary", "arbitrary", "arbitrary"), vmem_limit_bytes=VMEM_LIMIT),
        name="attn_prompt",
    )(slope_parts, q, k, v, selb)


def _ffn_kernel(x_ref, a_ref, cn_ref, gao_ref, wo_ref, gffn_ref, wg_ref, wu_ref, wd_ref, gfin_ref,
                y_ref):
    an = _rms(a_ref[...], gao_ref[...]).astype(BF16)
    mixed = jnp.concatenate([an, cn_ref[...]], axis=1)
    x1 = x_ref[...] + jnp.dot(mixed, wo_ref[...], preferred_element_type=F32)
    h2 = _rms(x1, gffn_ref[...]).astype(BF16)
    gate = jnp.dot(h2, wg_ref[...], preferred_element_type=F32)
    up = jnp.dot(h2, wu_ref[...], preferred_element_type=F32)
    act = (gate * (1.0 / (1.0 + jnp.exp(-gate))) * up).astype(BF16)
    x2 = x1 + jnp.dot(act, wd_ref[...], preferred_element_type=F32)
    y_ref[...] = _rms(x2, gfin_ref[...])


def _ffn(x, attn, cn, gao, wo, gffn, wg, wu, wd, gfin, tm):
    n, d = x.shape
    a = attn.shape[1]
    c = cn.shape[1]
    assert n % tm == 0
    row = lambda w: pl.BlockSpec((tm, w), lambda i: (i, 0))
    res = lambda arr: pl.BlockSpec(arr.shape, lambda i: (0, 0), pipeline_mode=pl.Buffered(1))
    return pl.pallas_call(
        _ffn_kernel,
        grid=(n // tm,),
        in_specs=[row(d), row(a), row(c), res(gao), res(wo), res(gffn), res(wg), res(wu), res(wd),
                  res(gfin)],
        out_specs=row(d),
        out_shape=jax.ShapeDtypeStruct((n, d), F32),
        compiler_params=pltpu.CompilerParams(dimension_semantics=("arbitrary",),
                                             vmem_limit_bytes=VMEM_LIMIT),
        name="ffn_%d" % n,
    )(x, attn, cn, gao, wo, gffn, wg, wu, wd, gfin)


def _mix_sample_kernel(x_ref, g_ref, w_ref, wc_ref, gco_ref, h0_ref, h1_ref,
                       q_ref, k_ref, v_ref, cn_ref, xc_ref):
    a = ATTN_WIDTH
    c = cn_ref.shape[1]
    h = _rms(x_ref[...], g_ref[...]).astype(BF16)
    z = jnp.dot(h, w_ref[...], preferred_element_type=F32)
    q_ref[...] = z[:, 0:a]
    k_ref[...] = z[:, a:2 * a]
    v_ref[...] = z[:, 2 * a:3 * a]
    gb = z[:, 3 * a:3 * a + c]
    xc = z[:, 3 * a + c:3 * a + 2 * c] * z[:, 3 * a + 2 * c:3 * a + 3 * c]
    y = wc_ref[0:1, :] * h0_ref[...] + wc_ref[1:2, :] * h1_ref[...] + wc_ref[2:3, :] * xc
    cn_ref[...] = _rms(gb * y, gco_ref[...]).astype(BF16)
    xc_ref[...] = xc


def _mix_sample(x, g, w_bf, wc, gco, h0, h1):
    n = x.shape[0]
    a = ATTN_WIDTH
    c = wc.shape[1]
    return pl.pallas_call(
        _mix_sample_kernel,
        out_shape=[jax.ShapeDtypeStruct((n, a), F32)] * 3
        + [jax.ShapeDtypeStruct((n, c), BF16), jax.ShapeDtypeStruct((n, c), F32)],
        compiler_params=pltpu.CompilerParams(vmem_limit_bytes=VMEM_LIMIT),
        name="mix_sample",
    )(x, g, w_bf, wc, gco, h0, h1)


N_PAGE_BUF = 16


def _page_gate_kernel(pt_ref, q_ref, ck_ref, idx_ref, buf_ref, sem_ref, part_ref):
    b = pl.program_id(0)
    n_seq, n_pages = pt_ref.shape
    nblk = n_pages // PAGES_PER_BLOCK
    total = n_seq * n_pages

    def page_copy(t, slot):
        pg = pt_ref[t // n_pages, t % n_pages]
        return pltpu.make_async_copy(ck_ref.at[0, pg], buf_ref.at[slot], sem_ref.at[slot])

    @pl.when(b == 0)
    def _():
        for t in range(N_PAGE_BUF):
            page_copy(t, t).start()

    qb = q_ref[...]

    def blk_body(n, _):
        t0 = (b * nblk + n) * PAGES_PER_BLOCK
        part = jnp.zeros((N_HEADS, SUBLANE, PAGE_SIZE), F32)
        for half in range(PAGES_PER_BLOCK):
            t = t0 + half
            slot = t % N_PAGE_BUF
            page_copy(t, slot).wait()
            prod = buf_ref[slot] * qb
            part = part + jnp.sum(
                prod.reshape(N_HEADS, HEAD_DIM // SUBLANE, SUBLANE, PAGE_SIZE), axis=1)

            @pl.when(t + N_PAGE_BUF < total)
            def _():
                page_copy(t + N_PAGE_BUF, slot).start()
        part_ref[n] = part
        return 0

    lax.fori_loop(0, nblk, blk_body, 0)
    g = jnp.sum(part_ref[...], axis=(2, 3))
    nidx = lax.broadcasted_iota(jnp.int32, g.shape, 0)
    picks = []
    for _ in range(MOBA_TOPK):
        mx = jnp.max(g, axis=0, keepdims=True)
        pick = jnp.min(jnp.where(g == mx, nidx, nblk), axis=0, keepdims=True)
        picks.append(pick)
        g = jnp.where(nidx == pick, -jnp.inf, g)
    idx_ref[...] = jnp.concatenate(picks, axis=0)


def _page_gate(page_table, q_bc, cache_kt):
    n_seq, n_pages = page_table.shape
    nblk = n_pages // PAGES_PER_BLOCK
    assert n_pages % PAGES_PER_BLOCK == 0 and N_PAGE_BUF % PAGES_PER_BLOCK == 0
    assert n_seq * n_pages >= N_PAGE_BUF and nblk >= MOBA_TOPK
    assert cache_kt.shape[2:] == (N_HEADS, HEAD_DIM, PAGE_SIZE) and HEAD_DIM % SUBLANE == 0
    grid_spec = pltpu.PrefetchScalarGridSpec(
        num_scalar_prefetch=1,
        grid=(n_seq,),
        in_specs=[pl.BlockSpec((None, N_HEADS, HEAD_DIM, PAGE_SIZE), lambda i, pt: (i, 0, 0, 0)),
                  pl.BlockSpec(memory_space=pl.ANY)],
        out_specs=pl.BlockSpec((None, MOBA_TOPK, N_HEADS), lambda i, pt: (i, 0, 0)),
        scratch_shapes=[pltpu.VMEM((N_PAGE_BUF, N_HEADS, HEAD_DIM, PAGE_SIZE), F32),
                        pltpu.SemaphoreType.DMA((N_PAGE_BUF,)),
                        pltpu.VMEM((nblk, N_HEADS, SUBLANE, PAGE_SIZE), F32)])
    return pl.pallas_call(
        _page_gate_kernel,
        grid_spec=grid_spec,
        out_shape=jax.ShapeDtypeStruct((n_seq, MOBA_TOPK, N_HEADS), jnp.int32),
        compiler_params=pltpu.CompilerParams(dimension_semantics=("arbitrary",),
                                             vmem_limit_bytes=VMEM_LIMIT),
        name="page_gate",
    )(page_table, q_bc, cache_kt)


def _attn_sample_kernel(pt_ref, idx_ref, q_ref, kn_ref, vn_ref, ck_ref, cv_ref, o_ref,
                        kbuf_ref, vbuf_ref, sem_ref, *, past_len, slopes):
    b = pl.program_id(0)
    n_slab = MOBA_TOPK * PAGES_PER_BLOCK

    cps = []
    for h in range(N_HEADS):
        for t in range(MOBA_TOPK):
            blk = idx_ref[b, t, h]
            for half in range(PAGES_PER_BLOCK):
                pg = pt_ref[b, blk * PAGES_PER_BLOCK + half]
                u = t * PAGES_PER_BLOCK + half
                cps.append(pltpu.make_async_copy(ck_ref.at[0, pg, h], kbuf_ref.at[h, u], sem_ref.at[0]))
                cps.append(pltpu.make_async_copy(cv_ref.at[0, pg, h], vbuf_ref.at[h, u], sem_ref.at[1]))
    for cp in cps:
        cp.start()
    for cp in cps:
        cp.wait()

    pos = lax.broadcasted_iota(jnp.int32, (1, PAGE_SIZE), 1)
    for h in range(N_HEADS):
        qh = q_ref[h] * ATTN_SCALE
        own = jnp.sum(qh * kn_ref[h], axis=0, keepdims=True)
        rows = []
        for t in range(MOBA_TOPK):
            blk = idx_ref[b, t, h]
            for half in range(PAGES_PER_BLOCK):
                u = t * PAGES_PER_BLOCK + half
                kpos = blk * MOBA_BLOCK + half * PAGE_SIZE + pos
                dist = (past_len - kpos).astype(F32)
                rows.append(jnp.sum(kbuf_ref[h, u] * qh, axis=0, keepdims=True)
                            - float(slopes[h]) * dist)
        mrow = functools.reduce(jnp.maximum, rows)
        m = jnp.maximum(jnp.max(mrow, axis=1, keepdims=True), own)
        p_own = jnp.exp(own - m)
        den = p_own
        acc = jnp.zeros((HEAD_DIM, PAGE_SIZE), F32)
        for u in range(n_slab):
            p = jnp.exp(rows[u] - m)
            den = den + jnp.sum(p, axis=1, keepdims=True)
            acc = acc + vbuf_ref[h, u] * p
        num = jnp.sum(acc, axis=1, keepdims=True) + (p_own * vn_ref[h])[:, 0:1]
        o_ref[:, h:h + 1] = num / den[:, 0:1]


def _attn_sample(page_table, idx, q_bc, kn_bc, vn_bc, cache_kt, cache_vt, past_len):
    n_seq = q_bc.shape[0]
    blk4 = pl.BlockSpec((None, N_HEADS, HEAD_DIM, PAGE_SIZE), lambda i, pt, ix: (i, 0, 0, 0))
    n_slab = MOBA_TOPK * PAGES_PER_BLOCK
    grid_spec = pltpu.PrefetchScalarGridSpec(
        num_scalar_prefetch=2,
        grid=(n_seq,),
        in_specs=[blk4, blk4, blk4, pl.BlockSpec(memory_space=pl.ANY), pl.BlockSpec(memory_space=pl.ANY)],
        out_specs=pl.BlockSpec((None, HEAD_DIM, N_HEADS), lambda i, pt, ix: (i, 0, 0)),
        scratch_shapes=[pltpu.VMEM((N_HEADS, n_slab, HEAD_DIM, PAGE_SIZE), F32),
                        pltpu.VMEM((N_HEADS, n_slab, HEAD_DIM, PAGE_SIZE), F32),
                        pltpu.SemaphoreType.DMA((2,))])
    kern = functools.partial(_attn_sample_kernel, past_len=past_len, slopes=_alibi_slopes_np())
    return pl.pallas_call(
        kern,
        grid_spec=grid_spec,
        out_shape=jax.ShapeDtypeStruct((n_seq, HEAD_DIM, N_HEADS), F32),
        compiler_params=pltpu.CompilerParams(dimension_semantics=("arbitrary",),
                                             vmem_limit_bytes=VMEM_LIMIT),
        name="attn_sample",
    )(page_table, idx, q_bc, kn_bc, vn_bc, cache_kt, cache_vt)


def kernel(x_prompt, x_sample, cache_k, cache_v, state_conv, page_table, norm_mix, w_mix, w_conv,
           norm_attn_out, norm_conv_out, w_o, norm_ffn, w_gate, w_up, w_down, norm_final):
    depth = w_mix.shape[0]
    assert depth == 1 and x_sample.shape[1] == 1
    b, s, d = x_prompt.shape
    n_seq = x_sample.shape[0]
    n_pages = page_table.shape[1]
    past_len = n_pages * PAGE_SIZE
    a = ATTN_WIDTH
    c = w_conv.shape[2]

    g_mix = norm_mix[0][None, :]
    g_ao = norm_attn_out[0][None, :]
    g_co = norm_conv_out[0][None, :]
    g_ffn = norm_ffn[0][None, :]
    g_fin = norm_final[None, :]
    w_mix_bf = w_mix[0].astype(BF16)
    w_o_bf = w_o[0].astype(BF16)
    w_gate_bf = w_gate[0].astype(BF16)
    w_up_bf = w_up[0].astype(BF16)
    w_down_bf = w_down[0].astype(BF16)
    wc = w_conv[0]

    sl = _alibi_slopes_np()
    hi = sl.astype(jnp.bfloat16).astype(np.float32)
    mid = (sl - hi).astype(jnp.bfloat16).astype(np.float32)
    lo = (sl - hi - mid).astype(jnp.bfloat16).astype(np.float32)
    slope_parts = jnp.asarray(np.concatenate([hi, mid, lo]))

    q_p, k_p, v_p, cn_p, selb_p, cst_p = _mix_prompt(x_prompt, g_mix, w_mix_bf, wc, g_co)
    attn_p = _attn_prompt(q_p, k_p, v_p, selb_p, slope_parts)
    y_p = _ffn(x_prompt.reshape(b * s, d), attn_p.reshape(b * s, a), cn_p.reshape(b * s, c),
               g_ao, w_o_bf, g_ffn, w_gate_bf, w_up_bf, w_down_bf, g_fin, tm=512).reshape(b, s, d)

    xs = x_sample[:, 0, :]
    h0 = state_conv[0, :, 0, :]
    h1 = state_conv[0, :, 1, :]
    q_s, k_s, v_s, cn_s, xc_s = _mix_sample(xs, g_mix, w_mix_bf, wc, g_co, h0, h1)
    hs = (N_HEADS, HEAD_DIM)
    lane_bc = lambda t: jnp.broadcast_to(t.reshape((n_seq,) + hs + (1,)), (n_seq,) + hs + (PAGE_SIZE,))
    q_bc, kn_bc, vn_bc = lane_bc(q_s), lane_bc(k_s), lane_bc(v_s)
    cache_kt = jnp.transpose(cache_k, (0, 1, 3, 4, 2))
    cache_vt = jnp.transpose(cache_v, (0, 1, 3, 4, 2))
    idx = _page_gate(page_table, q_bc, cache_kt)
    attn_s = _attn_sample(page_table, idx, q_bc, kn_bc, vn_bc, cache_kt, cache_vt, past_len)
    attn_s = jnp.transpose(attn_s, (0, 2, 1)).reshape(n_seq, a)
    y_s = _ffn(xs, attn_s, cn_s, g_ao, w_o_bf, g_ffn, w_gate_bf, w_up_bf, w_down_bf, g_fin, tm=n_seq)

    return (y_p, y_s[:, None, :],
            k_p.reshape((1, b, s) + hs), v_p.reshape((1, b, s) + hs), cst_p[None],
            k_s.reshape((1, n_seq, 1) + hs), v_s.reshape((1, n_seq, 1) + hs),
            jnp.stack([h1, xc_s], axis=1)[None])
```

```python
import functools

import numpy as np
import jax
import jax.numpy as jnp
from jax import lax
from jax.experimental import pallas as pl
from jax.experimental.pallas import tpu as pltpu

F32 = jnp.float32
BF16 = jnp.bfloat16

HEAD_DIM = 64
N_HEADS = 8
ATTN_WIDTH = N_HEADS * HEAD_DIM
MOBA_BLOCK = 256
MOBA_TOPK = 3
PAGE_SIZE = 128
PAGES_PER_BLOCK = MOBA_BLOCK // PAGE_SIZE
RMS_EPS = 1e-6
ATTN_SCALE = HEAD_DIM ** -0.5
LOG2E = 1.4426950408889634
NEG = -1e30
LANE = 128
SUBLANE = 8
VMEM_LIMIT = 56 * 1024 * 1024


def _alibi_slopes_np():
    return (2.0 ** (-8.0 * np.arange(1, N_HEADS + 1, dtype=np.float64) / N_HEADS)).astype(np.float32)


def _rms(x, g):
    y = x * lax.rsqrt(jnp.mean(x * x, axis=-1, keepdims=True) + RMS_EPS)
    return y * g


def _split_bf16(x):
    hi = x.astype(BF16)
    lo = (x - hi.astype(F32)).astype(BF16)
    return hi, lo


_DN_T = (((1,), (1,)), ((), ()))


def _mix_prompt_kernel(x_ref, g_ref, w_ref, wc_ref, gco_ref,
                       k_ref, v_ref, kb_ref, vt_ref, qt_ref, st_ref, cn_ref, cst_ref,
                       carry_ref, kmean_ref):
    s = pl.program_id(1)
    tm = x_ref.shape[0]
    a = ATTN_WIDTH
    c = cn_ref.shape[1]

    @pl.when(s == 0)
    def _():
        carry_ref[...] = jnp.zeros_like(carry_ref)
        kmean_ref[...] = jnp.zeros_like(kmean_ref)

    h = _rms(x_ref[...], g_ref[...]).astype(BF16)
    z = jnp.dot(h, w_ref[...], preferred_element_type=F32)
    q = z[:, 0:a]
    k = z[:, a:2 * a]
    v = z[:, 2 * a:3 * a]
    gb = z[:, 3 * a:3 * a + c]
    gc = z[:, 3 * a + c:3 * a + 2 * c]
    u = z[:, 3 * a + 2 * c:3 * a + 3 * c]

    k_ref[...] = k
    v_ref[...] = v
    kb_ref[...] = k.astype(BF16)
    vt_ref[...] = jnp.transpose(v).astype(BF16)
    qt_ref[...] = jnp.transpose(q * (ATTN_SCALE * LOG2E)).astype(BF16)

    xc = gc * u
    row = lax.broadcasted_iota(jnp.int32, xc.shape, 0)
    p0 = carry_ref[0:1, :]
    p1 = carry_ref[1:2, :]
    xc1 = jnp.where(row == 0, p1, pltpu.roll(xc, 1, axis=0))
    xc2 = jnp.where(row == 0, p0, jnp.where(row == 1, p1, pltpu.roll(xc, 2, axis=0)))
    y = wc_ref[0:1, :] * xc2 + wc_ref[1:2, :] * xc1 + wc_ref[2:3, :] * xc
    cn_ref[...] = _rms(gb * y, gco_ref[...]).astype(BF16)
    last2 = xc[tm - 2:tm, :]
    carry_ref[...] = last2
    cst_ref[...] = last2

    nb = kmean_ref.shape[0]
    km = kmean_ref[...]
    kmt = jnp.concatenate([km] * N_HEADS, axis=0)
    rr = lax.broadcasted_iota(jnp.int32, kmt.shape, 0) // nb
    cc = lax.broadcasted_iota(jnp.int32, kmt.shape, 1) // HEAD_DIM
    kmt = jnp.where(rr == cc, kmt, 0.0)
    qh, ql = _split_bf16(q)
    kh, kl = _split_bf16(kmt)
    gate = (lax.dot_general(qh, kh, _DN_T, preferred_element_type=F32)
            + lax.dot_general(ql, kh, _DN_T, preferred_element_type=F32)
            + lax.dot_general(qh, kl, _DN_T, preferred_element_type=F32))

    lane = lax.broadcasted_iota(jnp.int32, gate.shape, 1)
    jl = lane % nb
    valid = jl < s
    gm = jnp.where(valid, gate, -jnp.inf)
    cnt = jnp.zeros(gate.shape, F32)
    width = gate.shape[1]
    for sh in range(1, nb):
        fwd = pltpu.roll(gm, width - sh, axis=1)
        bwd = pltpu.roll(gm, nb - sh, axis=1)
        wrapped = (jl + sh) >= nb
        beats = jnp.where(wrapped, jnp.where(bwd >= gm, 1.0, 0.0), jnp.where(fwd > gm, 1.0, 0.0))
        cnt = cnt + beats
    selb = jnp.where(valid, jnp.where(cnt < MOBA_TOPK, 0.0, NEG), NEG)
    selb = jnp.where(jl == s, 0.0, selb)
    st_ref[...] = jnp.transpose(selb).astype(BF16)

    kmean_ref[pl.ds(s, 1), :] = jnp.sum(k, axis=0, keepdims=True) * (1.0 / MOBA_BLOCK)


def _mix_prompt(x, g, w_bf, wc, gco):
    b, s, d = x.shape
    tm = MOBA_BLOCK
    ns = s // tm
    a = ATTN_WIDTH
    c = wc.shape[1]
    assert s % tm == 0 and ns * N_HEADS == LANE and w_bf.shape[1] == 3 * a + 3 * c
    row_spec = lambda n: pl.BlockSpec((None, tm, n), lambda i, j: (i, j, 0))
    col_spec = lambda n: pl.BlockSpec((None, n, tm), lambda i, j: (i, 0, j))
    full = lambda shp: pl.BlockSpec(shp, lambda i, j: (0,) * len(shp))
    return pl.pallas_call(
        _mix_prompt_kernel,
        grid=(b, ns),
        in_specs=[row_spec(d), full((1, d)), full(w_bf.shape), full(wc.shape), full((1, c))],
        out_specs=[row_spec(a), row_spec(a), row_spec(a),
                   pl.BlockSpec((None, None, a, tm), lambda i, j: (i, j, 0, 0)),
                   col_spec(a), col_spec(LANE), row_spec(c),
                   pl.BlockSpec((None, 2, c), lambda i, j: (i, 0, 0))],
        out_shape=[jax.ShapeDtypeStruct((b, s, a), F32),
                   jax.ShapeDtypeStruct((b, s, a), F32),
                   jax.ShapeDtypeStruct((b, s, a), BF16),
                   jax.ShapeDtypeStruct((b, ns, a, tm), BF16),
                   jax.ShapeDtypeStruct((b, a, s), BF16),
                   jax.ShapeDtypeStruct((b, LANE, s), BF16),
                   jax.ShapeDtypeStruct((b, s, c), BF16),
                   jax.ShapeDtypeStruct((b, 2, c), F32)],
        scratch_shapes=[pltpu.VMEM((2, c), F32), pltpu.VMEM((ns, a), F32)],
        compiler_params=pltpu.CompilerParams(dimension_semantics=("arbitrary", "arbitrary"),
                                             vmem_limit_bytes=VMEM_LIMIT),
        name="mix_prompt",
    )(x, g, w_bf, wc, gco)


AUG_ROWS = 16


LOOKAHEAD = N_HEADS


def _attn_prompt_kernel(cst_ref, qt_ref, st_ref, kb_ref, vt_ref, o_ref, w_ref, sc_ref):
    i = pl.program_id(1)
    blk = MOBA_BLOCK
    hd = HEAD_DIM

    zq = jnp.zeros((hd, blk), BF16)
    zpad = jnp.zeros((LANE - 2 * AUG_ROWS, blk), BF16)
    for h in range(N_HEADS):
        qh = qt_ref[h * hd:(h + 1) * hd, :]
        qpart = [qh, zq] if h % 2 == 0 else [zq, qh]
        w_ref[h] = jnp.concatenate(
            qpart + [st_ref[h * AUG_ROWS:(h + 1) * AUG_ROWS, :], cst_ref[h], zpad], axis=0)

    elane = lax.broadcasted_iota(jnp.int32, (blk, LANE), 1)
    erow = lax.broadcasted_iota(jnp.int32, (blk, LANE), 0).astype(F32)
    ebase = jnp.where((elane >= AUG_ROWS) & (elane < AUG_ROWS + 3), erow, 0.0)
    rk = lax.broadcasted_iota(jnp.int32, (blk, blk), 0)
    rq = lax.broadcasted_iota(jnp.int32, (blk, blk), 1)

    def key_aug(j):
        off = ((j - i) * blk).astype(F32)
        return jnp.where(elane == j, 1.0,
                         jnp.where((elane >= AUG_ROWS + 3) & (elane < AUG_ROWS + 6), off, ebase)).astype(BF16)

    def scores(kbj, e, h):
        rhs = jnp.concatenate([kbj[:, (h // 2) * LANE:(h // 2 + 1) * LANE], e], axis=1)
        return jnp.dot(rhs, w_ref[h], preferred_element_type=F32)

    def block(j, state, nxt, diagonal):
        kbj = kb_ref[j]
        vtj = vt_ref[j]
        e = key_aug(j)
        if nxt is not None:
            kbn = kb_ref[nxt]
            en = key_aug(nxt)
        out = []
        for h in range(N_HEADS):
            st = sc_ref[h]
            ahead = h + LOOKAHEAD
            if ahead < N_HEADS:
                sc_ref[ahead] = scores(kbj, e, ahead)
            elif nxt is not None:
                sc_ref[ahead - N_HEADS] = scores(kbn, en, ahead - N_HEADS)
            if diagonal:
                st = st + jnp.where(rk <= rq, 0.0, NEG)
            m_old, l_old, acc_old = state[3 * h:3 * h + 3]
            m_new = jnp.maximum(m_old, jnp.max(st, axis=0, keepdims=True))
            alpha = jnp.exp2(m_old - m_new)
            p = jnp.exp2(st - m_new)
            l_new = alpha * l_old + jnp.sum(p, axis=0, keepdims=True)
            acc_new = alpha * acc_old + jnp.dot(vtj[h * hd:(h + 1) * hd, :], p.astype(BF16),
                                                preferred_element_type=F32)
            out += [m_new, l_new, acc_new]
        return tuple(out)

    kb0 = kb_ref[0]
    e0 = key_aug(0)
    for h in range(LOOKAHEAD):
        sc_ref[h] = scores(kb0, e0, h)
    state = []
    for h in range(N_HEADS):
        state += [jnp.full((1, blk), -jnp.inf, F32), jnp.zeros((1, blk), F32), jnp.zeros((hd, blk), F32)]
    state = lax.fori_loop(0, i, lambda j, s: block(j, s, j + 1, False), tuple(state))
    state = block(i, state, None, True)
    o_t = jnp.concatenate([state[3 * h + 2] / state[3 * h + 1] for h in range(N_HEADS)], axis=0)
    o_ref[...] = jnp.transpose(o_t)


def _attn_prompt(qt, st, kb, vt, cst):
    b, nb, blk, a = kb.shape
    assert blk == MOBA_BLOCK and nb <= AUG_ROWS and a == ATTN_WIDTH and st.shape[1] == N_HEADS * AUG_ROWS
    return pl.pallas_call(
        _attn_prompt_kernel,
        grid=(b, nb),
        in_specs=[pl.BlockSpec(cst.shape, lambda bi, i: (0, 0, 0)),
                  pl.BlockSpec((None, a, blk), lambda bi, i: (bi, 0, i)),
                  pl.BlockSpec((None, N_HEADS * AUG_ROWS, blk), lambda bi, i: (bi, 0, i)),
                  pl.BlockSpec((None, nb, blk, a), lambda bi, i: (bi, 0, 0, 0)),
                  pl.BlockSpec((None, nb, a, blk), lambda bi, i: (bi, 0, 0, 0))],
        out_specs=pl.BlockSpec((None, blk, a), lambda bi, i: (bi, i, 0)),
        out_shape=jax.ShapeDtypeStruct((b, nb * blk, a), F32),
        scratch_shapes=[pltpu.VMEM((N_HEADS, 2 * LANE, blk), BF16),
                        pltpu.VMEM((N_HEADS, blk, blk), F32)],
        compiler_params=pltpu.CompilerParams(
            dimension_semantics=("arbitrary", "arbitrary"), vmem_limit_bytes=VMEM_LIMIT),
        name="attn_prompt",
    )(cst, qt, st, kb, vt)


def _ffn_kernel(x_ref, a_ref, cn_ref, gao_ref, wo_ref, gffn_ref, wg_ref, wu_ref, wd_ref, gfin_ref,
                y_ref):
    an = _rms(a_ref[...], gao_ref[...]).astype(BF16)
    mixed = jnp.concatenate([an, cn_ref[...]], axis=1)
    x1 = x_ref[...] + jnp.dot(mixed, wo_ref[...], preferred_element_type=F32)
    h2 = _rms(x1, gffn_ref[...]).astype(BF16)
    gate = jnp.dot(h2, wg_ref[...], preferred_element_type=F32)
    up = jnp.dot(h2, wu_ref[...], preferred_element_type=F32)
    act = (gate * (1.0 / (1.0 + jnp.exp(-gate))) * up).astype(BF16)
    x2 = x1 + jnp.dot(act, wd_ref[...], preferred_element_type=F32)
    y_ref[...] = _rms(x2, gfin_ref[...])


def _ffn(x, attn, cn, gao, wo, gffn, wg, wu, wd, gfin, tm):
    n, d = x.shape
    a = attn.shape[1]
    c = cn.shape[1]
    assert n % tm == 0
    row = lambda w: pl.BlockSpec((tm, w), lambda i: (i, 0))
    res = lambda arr: pl.BlockSpec(arr.shape, lambda i: (0, 0), pipeline_mode=pl.Buffered(1))
    return pl.pallas_call(
        _ffn_kernel,
        grid=(n // tm,),
        in_specs=[row(d), row(a), row(c), res(gao), res(wo), res(gffn), res(wg), res(wu), res(wd),
                  res(gfin)],
        out_specs=row(d),
        out_shape=jax.ShapeDtypeStruct((n, d), F32),
        compiler_params=pltpu.CompilerParams(dimension_semantics=("arbitrary",),
                                             vmem_limit_bytes=VMEM_LIMIT),
        name="ffn_%d" % n,
    )(x, attn, cn, gao, wo, gffn, wg, wu, wd, gfin)


def _mix_sample_kernel(x_ref, g_ref, w_ref, wc_ref, gco_ref, h0_ref, h1_ref,
                       q_ref, k_ref, v_ref, cn_ref, xc_ref):
    a = ATTN_WIDTH
    c = cn_ref.shape[1]
    h = _rms(x_ref[...], g_ref[...]).astype(BF16)
    z = jnp.dot(h, w_ref[...], preferred_element_type=F32)
    q_ref[...] = z[:, 0:a]
    k_ref[...] = z[:, a:2 * a]
    v_ref[...] = z[:, 2 * a:3 * a]
    gb = z[:, 3 * a:3 * a + c]
    xc = z[:, 3 * a + c:3 * a + 2 * c] * z[:, 3 * a + 2 * c:3 * a + 3 * c]
    y = wc_ref[0:1, :] * h0_ref[...] + wc_ref[1:2, :] * h1_ref[...] + wc_ref[2:3, :] * xc
    cn_ref[...] = _rms(gb * y, gco_ref[...]).astype(BF16)
    xc_ref[...] = xc


def _mix_sample(x, g, w_bf, wc, gco, h0, h1):
    n = x.shape[0]
    a = ATTN_WIDTH
    c = wc.shape[1]
    return pl.pallas_call(
        _mix_sample_kernel,
        out_shape=[jax.ShapeDtypeStruct((n, a), F32)] * 3
        + [jax.ShapeDtypeStruct((n, c), BF16), jax.ShapeDtypeStruct((n, c), F32)],
        compiler_params=pltpu.CompilerParams(vmem_limit_bytes=VMEM_LIMIT),
        name="mix_sample",
    )(x, g, w_bf, wc, gco, h0, h1)


N_PAGE_BUF = 16


def _page_gate_kernel(pt_ref, q_ref, ck_ref, idx_ref, buf_ref, sem_ref, part_ref):
    b = pl.program_id(0)
    n_seq, n_pages = pt_ref.shape
    nblk = n_pages // PAGES_PER_BLOCK
    total = n_seq * n_pages

    def page_copy(t, slot):
        pg = pt_ref[t // n_pages, t % n_pages]
        return pltpu.make_async_copy(ck_ref.at[0, pg], buf_ref.at[slot], sem_ref.at[slot])

    @pl.when(b == 0)
    def _():
        for t in range(N_PAGE_BUF):
            page_copy(t, t).start()

    qb = q_ref[...]

    def blk_body(n, _):
        t0 = (b * nblk + n) * PAGES_PER_BLOCK
        part = jnp.zeros((N_HEADS, SUBLANE, PAGE_SIZE), F32)
        for half in range(PAGES_PER_BLOCK):
            t = t0 + half
            slot = t % N_PAGE_BUF
            page_copy(t, slot).wait()
            prod = buf_ref[slot] * qb
            part = part + jnp.sum(
                prod.reshape(N_HEADS, HEAD_DIM // SUBLANE, SUBLANE, PAGE_SIZE), axis=1)

            @pl.when(t + N_PAGE_BUF < total)
            def _():
                page_copy(t + N_PAGE_BUF, slot).start()
        part_ref[n] = part
        return 0

    lax.fori_loop(0, nblk, blk_body, 0)
    g = jnp.sum(part_ref[...], axis=(2, 3))
    nidx = lax.broadcasted_iota(jnp.int32, g.shape, 0)
    picks = []
    for _ in range(MOBA_TOPK):
        mx = jnp.max(g, axis=0, keepdims=True)
        pick = jnp.min(jnp.where(g == mx, nidx, nblk), axis=0, keepdims=True)
        picks.append(pick)
        g = jnp.where(nidx == pick, -jnp.inf, g)
    idx_ref[...] = jnp.concatenate(picks, axis=0)


def _page_gate(page_table, q_bc, cache_kt):
    n_seq, n_pages = page_table.shape
    nblk = n_pages // PAGES_PER_BLOCK
    assert n_pages % PAGES_PER_BLOCK == 0 and N_PAGE_BUF % PAGES_PER_BLOCK == 0
    assert n_seq * n_pages >= N_PAGE_BUF and nblk >= MOBA_TOPK
    assert cache_kt.shape[2:] == (N_HEADS, HEAD_DIM, PAGE_SIZE) and HEAD_DIM % SUBLANE == 0
    grid_spec = pltpu.PrefetchScalarGridSpec(
        num_scalar_prefetch=1,
        grid=(n_seq,),
        in_specs=[pl.BlockSpec((None, N_HEADS, HEAD_DIM, PAGE_SIZE), lambda i, pt: (i, 0, 0, 0)),
                  pl.BlockSpec(memory_space=pl.ANY)],
        out_specs=pl.BlockSpec((None, MOBA_TOPK, N_HEADS), lambda i, pt: (i, 0, 0)),
        scratch_shapes=[pltpu.VMEM((N_PAGE_BUF, N_HEADS, HEAD_DIM, PAGE_SIZE), F32),
                        pltpu.SemaphoreType.DMA((N_PAGE_BUF,)),
                        pltpu.VMEM((nblk, N_HEADS, SUBLANE, PAGE_SIZE), F32)])
    return pl.pallas_call(
        _page_gate_kernel,
        grid_spec=grid_spec,
        out_shape=jax.ShapeDtypeStruct((n_seq, MOBA_TOPK, N_HEADS), jnp.int32),
        compiler_params=pltpu.CompilerParams(dimension_semantics=("arbitrary",),
                                             vmem_limit_bytes=VMEM_LIMIT),
        name="page_gate",
    )(page_table, q_bc, cache_kt)


def _attn_sample_kernel(pt_ref, idx_ref, q_ref, kn_ref, vn_ref, ck_ref, cv_ref, o_ref,
                        kbuf_ref, vbuf_ref, sem_ref, *, past_len, slopes):
    b = pl.program_id(0)
    n_seq = pl.num_programs(0)
    n_slab = MOBA_TOPK * PAGES_PER_BLOCK
    slot = b % 2

    def copies(seq, sl):
        cps = []
        for h in range(N_HEADS):
            for t in range(MOBA_TOPK):
                blk = idx_ref[seq, t, h]
                for half in range(PAGES_PER_BLOCK):
                    pg = pt_ref[seq, blk * PAGES_PER_BLOCK + half]
                    u = t * PAGES_PER_BLOCK + half
                    cps.append(pltpu.make_async_copy(ck_ref.at[0, pg, h], kbuf_ref.at[sl, h, u],
                                                     sem_ref.at[0, sl]))
                    cps.append(pltpu.make_async_copy(cv_ref.at[0, pg, h], vbuf_ref.at[sl, h, u],
                                                     sem_ref.at[1, sl]))
        return cps

    @pl.when(b == 0)
    def _():
        for cp in copies(0, 0):
            cp.start()

    @pl.when(b + 1 < n_seq)
    def _():
        for cp in copies(b + 1, 1 - slot):
            cp.start()

    for cp in copies(b, slot):
        cp.wait()

    pos = lax.broadcasted_iota(jnp.int32, (1, PAGE_SIZE), 1)
    for h in range(N_HEADS):
        qh = q_ref[h] * ATTN_SCALE
        own = jnp.sum(qh * kn_ref[h], axis=0, keepdims=True)
        rows = []
        for t in range(MOBA_TOPK):
            blk = idx_ref[b, t, h]
            for half in range(PAGES_PER_BLOCK):
                u = t * PAGES_PER_BLOCK + half
                kpos = blk * MOBA_BLOCK + half * PAGE_SIZE + pos
                dist = (past_len - kpos).astype(F32)
                rows.append(jnp.sum(kbuf_ref[slot, h, u] * qh, axis=0, keepdims=True)
                            - float(slopes[h]) * dist)
        mrow = functools.reduce(jnp.maximum, rows)
        m = jnp.maximum(jnp.max(mrow, axis=1, keepdims=True), own)
        p_own = jnp.exp(own - m)
        den = p_own
        acc = jnp.zeros((HEAD_DIM, PAGE_SIZE), F32)
        for u in range(n_slab):
            p = jnp.exp(rows[u] - m)
            den = den + jnp.sum(p, axis=1, keepdims=True)
            acc = acc + vbuf_ref[slot, h, u] * p
        num = jnp.sum(acc, axis=1, keepdims=True) + (p_own * vn_ref[h])[:, 0:1]
        o_ref[:, h:h + 1] = num / den[:, 0:1]


def _attn_sample(page_table, idx, q_bc, kn_bc, vn_bc, cache_kt, cache_vt, past_len):
    n_seq = q_bc.shape[0]
    blk4 = pl.BlockSpec((None, N_HEADS, HEAD_DIM, PAGE_SIZE), lambda i, pt, ix: (i, 0, 0, 0))
    n_slab = MOBA_TOPK * PAGES_PER_BLOCK
    grid_spec = pltpu.PrefetchScalarGridSpec(
        num_scalar_prefetch=2,
        grid=(n_seq,),
        in_specs=[blk4, blk4, blk4, pl.BlockSpec(memory_space=pl.ANY), pl.BlockSpec(memory_space=pl.ANY)],
        out_specs=pl.BlockSpec((None, HEAD_DIM, N_HEADS), lambda i, pt, ix: (i, 0, 0)),
        scratch_shapes=[pltpu.VMEM((2, N_HEADS, n_slab, HEAD_DIM, PAGE_SIZE), F32),
                        pltpu.VMEM((2, N_HEADS, n_slab, HEAD_DIM, PAGE_SIZE), F32),
                        pltpu.SemaphoreType.DMA((2, 2))])
    kern = functools.partial(_attn_sample_kernel, past_len=past_len, slopes=_alibi_slopes_np())
    return pl.pallas_call(
        kern,
        grid_spec=grid_spec,
        out_shape=jax.ShapeDtypeStruct((n_seq, HEAD_DIM, N_HEADS), F32),
        compiler_params=pltpu.CompilerParams(dimension_semantics=("arbitrary",),
                                             vmem_limit_bytes=VMEM_LIMIT),
        name="attn_sample",
    )(page_table, idx, q_bc, kn_bc, vn_bc, cache_kt, cache_vt)


def kernel(x_prompt, x_sample, cache_k, cache_v, state_conv, page_table, norm_mix, w_mix, w_conv,
           norm_attn_out, norm_conv_out, w_o, norm_ffn, w_gate, w_up, w_down, norm_final):
    depth = w_mix.shape[0]
    assert depth == 1 and x_sample.shape[1] == 1
    b, s, d = x_prompt.shape
    n_seq = x_sample.shape[0]
    n_pages = page_table.shape[1]
    past_len = n_pages * PAGE_SIZE
    a = ATTN_WIDTH
    c = w_conv.shape[2]

    g_mix = norm_mix[0][None, :]
    g_ao = norm_attn_out[0][None, :]
    g_co = norm_conv_out[0][None, :]
    g_ffn = norm_ffn[0][None, :]
    g_fin = norm_final[None, :]
    w_mix_bf = w_mix[0].astype(BF16)
    w_o_bf = w_o[0].astype(BF16)
    w_gate_bf = w_gate[0].astype(BF16)
    w_up_bf = w_up[0].astype(BF16)
    w_down_bf = w_down[0].astype(BF16)
    wc = w_conv[0]

    sl = (_alibi_slopes_np().astype(np.float64) * LOG2E).astype(np.float32)
    hi = sl.astype(jnp.bfloat16).astype(np.float32)
    mid = (sl - hi).astype(jnp.bfloat16).astype(np.float32)
    lo = (sl - hi - mid).astype(jnp.bfloat16).astype(np.float32)
    cst_np = np.zeros((N_HEADS, AUG_ROWS, MOBA_BLOCK), np.float32)
    for r, piece in enumerate((hi, mid, lo, hi, mid, lo)):
        cst_np[:, r, :] = piece[:, None]
    cst = jnp.asarray(cst_np, dtype=BF16)

    k_p, v_p, kb_p, vt_p, qt_p, st_p, cn_p, cst_p = _mix_prompt(x_prompt, g_mix, w_mix_bf, wc, g_co)
    attn_p = _attn_prompt(qt_p, st_p, kb_p.reshape(b, s // MOBA_BLOCK, MOBA_BLOCK, a), vt_p, cst)
    y_p = _ffn(x_prompt.reshape(b * s, d), attn_p.reshape(b * s, a), cn_p.reshape(b * s, c),
               g_ao, w_o_bf, g_ffn, w_gate_bf, w_up_bf, w_down_bf, g_fin, tm=512).reshape(b, s, d)

    xs = x_sample[:, 0, :]
    h0 = state_conv[0, :, 0, :]
    h1 = state_conv[0, :, 1, :]
    q_s, k_s, v_s, cn_s, xc_s = _mix_sample(xs, g_mix, w_mix_bf, wc, g_co, h0, h1)
    hs = (N_HEADS, HEAD_DIM)
    lane_bc = lambda t: jnp.broadcast_to(t.reshape((n_seq,) + hs + (1,)), (n_seq,) + hs + (PAGE_SIZE,))
    q_bc, kn_bc, vn_bc = lane_bc(q_s), lane_bc(k_s), lane_bc(v_s)
    cache_kt = jnp.transpose(cache_k, (0, 1, 3, 4, 2))
    cache_vt = jnp.transpose(cache_v, (0, 1, 3, 4, 2))
    idx = _page_gate(page_table, q_bc, cache_kt)
    attn_s = _attn_sample(page_table, idx, q_bc, kn_bc, vn_bc, cache_kt, cache_vt, past_len)
    attn_s = jnp.transpose(attn_s, (0, 2, 1)).reshape(n_seq, a)
    y_s = _ffn(xs, attn_s, cn_s, g_ao, w_o_bf, g_ffn, w_gate_bf, w_up_bf, w_down_bf, g_fin, tm=n_seq)

    return (y_p, y_s[:, None, :],
            k_p.reshape((1, b, s) + hs), v_p.reshape((1, b, s) + hs), cst_p[None],
            k_s.reshape((1, n_seq, 1) + hs), v_s.reshape((1, n_seq, 1) + hs),
            jnp.stack([h1, xc_s], axis=1)[None])
```

```python
import functools

import numpy as np
import jax
import jax.numpy as jnp
from jax import lax
from jax.experimental import pallas as pl
from jax.experimental.pallas import tpu as pltpu

F32 = jnp.float32
BF16 = jnp.bfloat16

HEAD_DIM = 64
N_HEADS = 8
ATTN_WIDTH = N_HEADS * HEAD_DIM
MOBA_BLOCK = 256
MOBA_TOPK = 3
PAGE_SIZE = 128
PAGES_PER_BLOCK = MOBA_BLOCK // PAGE_SIZE
RMS_EPS = 1e-6
ATTN_SCALE = HEAD_DIM ** -0.5
LOG2E = 1.4426950408889634
NEG = -1e30
LANE = 128
SUBLANE = 8
VMEM_LIMIT = 56 * 1024 * 1024


def _alibi_slopes_np():
    return (2.0 ** (-8.0 * np.arange(1, N_HEADS + 1, dtype=np.float64) / N_HEADS)).astype(np.float32)


def _rms(x, g):
    y = x * lax.rsqrt(jnp.mean(x * x, axis=-1, keepdims=True) + RMS_EPS)
    return y * g


def _split_bf16(x):
    hi = x.astype(BF16)
    lo = (x - hi.astype(F32)).astype(BF16)
    return hi, lo


_DN_T = (((1,), (1,)), ((), ()))


MIX_CHUNK = 512


def _mix_project(x_ref, g_ref, w_ref, z_ref, finish=()):
    h = _rms(x_ref[...], g_ref[...]).astype(BF16)
    finish = iter(finish)
    for c0 in range(0, w_ref.shape[1], MIX_CHUNK):
        z_ref[:, c0:c0 + MIX_CHUNK] = jnp.dot(h, w_ref[:, c0:c0 + MIX_CHUNK],
                                              preferred_element_type=F32)
        next(finish, None)
    for _ in finish:
        pass


def _mix_finish(s, z_ref, wc_ref, gco_ref,
                k_ref, v_ref, kb_ref, vt_ref, qt_ref, st_ref, cn_ref, cst_ref, carry_ref, kmean_ref):
    tm = z_ref.shape[0]
    a = ATTN_WIDTH
    c = cn_ref.shape[1]
    first = s == 0

    k = z_ref[:, a:2 * a]
    v = z_ref[:, 2 * a:3 * a]
    k_ref[...] = k
    v_ref[...] = v
    kb_ref[...] = k.astype(BF16)
    vt = jnp.transpose(v)
    vt_ref[...] = vt.astype(BF16)
    ksum = jnp.sum(k, axis=0, keepdims=True) * (1.0 / MOBA_BLOCK)
    yield

    q = z_ref[:, 0:a]
    qt = jnp.transpose(q * (ATTN_SCALE * LOG2E))
    qt_ref[...] = qt.astype(BF16)
    yield

    gb = z_ref[:, 3 * a:3 * a + c]
    gc = z_ref[:, 3 * a + c:3 * a + 2 * c]
    u = z_ref[:, 3 * a + 2 * c:3 * a + 3 * c]
    xc = gc * u
    row = lax.broadcasted_iota(jnp.int32, xc.shape, 0)
    p0 = jnp.where(first, 0.0, carry_ref[0:1, :])
    p1 = jnp.where(first, 0.0, carry_ref[1:2, :])
    xc1 = jnp.where(row == 0, p1, pltpu.roll(xc, 1, axis=0))
    xc2 = jnp.where(row == 0, p0, jnp.where(row == 1, p1, pltpu.roll(xc, 2, axis=0)))
    y = wc_ref[0:1, :] * xc2 + wc_ref[1:2, :] * xc1 + wc_ref[2:3, :] * xc
    cn = _rms(gb * y, gco_ref[...])
    cn_ref[...] = cn.astype(BF16)
    last2 = xc[tm - 2:tm, :]
    carry_ref[...] = last2
    cst_ref[...] = last2
    yield

    nb = kmean_ref.shape[0]
    krow = lax.broadcasted_iota(jnp.int32, kmean_ref.shape, 0)
    km = jnp.where(krow < s, kmean_ref[...], 0.0)
    kmt = jnp.concatenate([km] * N_HEADS, axis=0)
    rr = lax.broadcasted_iota(jnp.int32, kmt.shape, 0) // nb
    cc = lax.broadcasted_iota(jnp.int32, kmt.shape, 1) // HEAD_DIM
    kmt = jnp.where(rr == cc, kmt, 0.0)
    qh, ql = _split_bf16(q)
    kh, kl = _split_bf16(kmt)
    gate = (lax.dot_general(qh, kh, _DN_T, preferred_element_type=F32)
            + lax.dot_general(ql, kh, _DN_T, preferred_element_type=F32)
            + lax.dot_general(qh, kl, _DN_T, preferred_element_type=F32))

    lane = lax.broadcasted_iota(jnp.int32, gate.shape, 1)
    jl = lane % nb
    valid = jl < s
    gm = jnp.where(valid, gate, -jnp.inf)
    cnt = jnp.zeros(gate.shape, F32)
    width = gate.shape[1]
    yield
    for sh in range(1, nb):
        fwd = pltpu.roll(gm, width - sh, axis=1)
        bwd = pltpu.roll(gm, nb - sh, axis=1)
        wrapped = (jl + sh) >= nb
        beats = jnp.where(wrapped, jnp.where(bwd >= gm, 1.0, 0.0), jnp.where(fwd > gm, 1.0, 0.0))
        cnt = cnt + beats
        if sh % 8 == 0:
            yield
    selb = jnp.where(valid, jnp.where(cnt < MOBA_TOPK, 0.0, NEG), NEG)
    selb = jnp.where(jl == s, 0.0, selb)
    st_ref[...] = jnp.transpose(selb).astype(BF16)

    kmean_ref[pl.ds(s, 1), :] = ksum


def _mix_prompt_kernel(x_ref, g_ref, w_ref, wc_ref, gco_ref,
                       k_ref, v_ref, kb_ref, vt_ref, qt_ref, st_ref, cn_ref, cst_ref,
                       carry_ref, kmean_ref, za_ref, zb_ref, *, tiles_per_seq):
    t = pl.program_id(0)
    s_prev = (t + tiles_per_seq - 1) % tiles_per_seq
    outs = (k_ref, v_ref, kb_ref, vt_ref, qt_ref, st_ref, cn_ref, cst_ref, carry_ref, kmean_ref)

    @pl.when(t == 0)
    def _():
        carry_ref[...] = jnp.zeros_like(carry_ref)
        kmean_ref[...] = jnp.zeros_like(kmean_ref)
        _mix_project(x_ref, g_ref, w_ref, za_ref)

    @pl.when(t % 2 == 1)
    def _():
        _mix_project(x_ref, g_ref, w_ref, zb_ref, _mix_finish(s_prev, za_ref, wc_ref, gco_ref, *outs))

    @pl.when((t % 2 == 0) & (t > 0))
    def _():
        _mix_project(x_ref, g_ref, w_ref, za_ref, _mix_finish(s_prev, zb_ref, wc_ref, gco_ref, *outs))


def _mix_prompt(x, g, w_bf, wc, gco):
    b, s, d = x.shape
    tm = MOBA_BLOCK
    ns = s // tm
    nt = b * ns
    a = ATTN_WIDTH
    c = wc.shape[1]
    assert s % tm == 0 and ns * N_HEADS == LANE and w_bf.shape[1] == 3 * a + 3 * c and nt % 2 == 0
    tin = lambda t: jnp.minimum(t, nt - 1)
    tout = lambda t: jnp.maximum(t - 1, 0)
    row_in = lambda n: pl.BlockSpec((None, tm, n), lambda t: (tin(t) // ns, tin(t) % ns, 0))
    row_spec = lambda n: pl.BlockSpec((None, tm, n), lambda t: (tout(t) // ns, tout(t) % ns, 0))
    col_spec = lambda n: pl.BlockSpec((None, n, tm), lambda t: (tout(t) // ns, 0, tout(t) % ns))
    full = lambda shp: pl.BlockSpec(shp, lambda t: (0,) * len(shp))
    return pl.pallas_call(
        functools.partial(_mix_prompt_kernel, tiles_per_seq=ns),
        grid=(nt + 1,),
        in_specs=[row_in(d), full((1, d)), full(w_bf.shape), full(wc.shape), full((1, c))],
        out_specs=[row_spec(a), row_spec(a), row_spec(a),
                   pl.BlockSpec((None, None, a, tm), lambda t: (tout(t) // ns, tout(t) % ns, 0, 0)),
                   col_spec(a), col_spec(LANE), row_spec(c),
                   pl.BlockSpec((None, 2, c), lambda t: (tout(t) // ns, 0, 0))],
        out_shape=[jax.ShapeDtypeStruct((b, s, a), F32),
                   jax.ShapeDtypeStruct((b, s, a), F32),
                   jax.ShapeDtypeStruct((b, s, a), BF16),
                   jax.ShapeDtypeStruct((b, ns, a, tm), BF16),
                   jax.ShapeDtypeStruct((b, a, s), BF16),
                   jax.ShapeDtypeStruct((b, LANE, s), BF16),
                   jax.ShapeDtypeStruct((b, s, c), BF16),
                   jax.ShapeDtypeStruct((b, 2, c), F32)],
        scratch_shapes=[pltpu.VMEM((2, c), F32), pltpu.VMEM((ns, a), F32),
                        pltpu.VMEM((tm, w_bf.shape[1]), F32), pltpu.VMEM((tm, w_bf.shape[1]), F32)],
        compiler_params=pltpu.CompilerParams(dimension_semantics=("arbitrary",),
                                             vmem_limit_bytes=VMEM_LIMIT),
        name="mix_prompt",
    )(x, g, w_bf, wc, gco)


AUG_ROWS = 16


LOOKAHEAD = N_HEADS


def _attn_prompt_kernel(cst_ref, qt_ref, st_ref, kb_ref, vt_ref, o_ref, w_ref, sc_ref):
    i = pl.program_id(1)
    blk = MOBA_BLOCK
    hd = HEAD_DIM

    zq = jnp.zeros((hd, blk), BF16)
    zpad = jnp.zeros((LANE - 2 * AUG_ROWS, blk), BF16)
    for h in range(N_HEADS):
        qh = qt_ref[h * hd:(h + 1) * hd, :]
        qpart = [qh, zq] if h % 2 == 0 else [zq, qh]
        w_ref[h] = jnp.concatenate(
            qpart + [st_ref[h * AUG_ROWS:(h + 1) * AUG_ROWS, :], cst_ref[h], zpad], axis=0)

    elane = lax.broadcasted_iota(jnp.int32, (blk, LANE), 1)
    erow = lax.broadcasted_iota(jnp.int32, (blk, LANE), 0).astype(F32)
    ebase = jnp.where((elane >= AUG_ROWS) & (elane < AUG_ROWS + 3), erow, 0.0)
    rk = lax.broadcasted_iota(jnp.int32, (blk, blk), 0)
    rq = lax.broadcasted_iota(jnp.int32, (blk, blk), 1)

    def key_aug(j):
        off = ((j - i) * blk).astype(F32)
        return jnp.where(elane == j, 1.0,
                         jnp.where((elane >= AUG_ROWS + 3) & (elane < AUG_ROWS + 6), off, ebase)).astype(BF16)

    def scores(kbj, e, h):
        rhs = jnp.concatenate([kbj[:, (h // 2) * LANE:(h // 2 + 1) * LANE], e], axis=1)
        return jnp.dot(rhs, w_ref[h], preferred_element_type=F32)

    def block(j, state, nxt, diagonal):
        kbj = kb_ref[j]
        vtj = vt_ref[j]
        e = key_aug(j)
        if nxt is not None:
            kbn = kb_ref[nxt]
            en = key_aug(nxt)
        out = []
        for h in range(N_HEADS):
            st = sc_ref[h]
            ahead = h + LOOKAHEAD
            if ahead < N_HEADS:
                sc_ref[ahead] = scores(kbj, e, ahead)
            elif nxt is not None:
                sc_ref[ahead - N_HEADS] = scores(kbn, en, ahead - N_HEADS)
            if diagonal:
                st = st + jnp.where(rk <= rq, 0.0, NEG)
            m_old, l_old, acc_old = state[3 * h:3 * h + 3]
            m_new = jnp.maximum(m_old, jnp.max(st, axis=0, keepdims=True))
            alpha = jnp.exp2(m_old - m_new)
            p = jnp.exp2(st - m_new)
            l_new = alpha * l_old + jnp.sum(p, axis=0, keepdims=True)
            acc_new = alpha * acc_old + jnp.dot(vtj[h * hd:(h + 1) * hd, :], p.astype(BF16),
                                                preferred_element_type=F32)
            out += [m_new, l_new, acc_new]
        return tuple(out)

    kb0 = kb_ref[0]
    e0 = key_aug(0)
    for h in range(LOOKAHEAD):
        sc_ref[h] = scores(kb0, e0, h)
    state = []
    for h in range(N_HEADS):
        state += [jnp.full((1, blk), -jnp.inf, F32), jnp.zeros((1, blk), F32), jnp.zeros((hd, blk), F32)]
    state = lax.fori_loop(0, i, lambda j, s: block(j, s, j + 1, False), tuple(state))
    state = block(i, state, None, True)
    o_t = jnp.concatenate([state[3 * h + 2] / state[3 * h + 1] for h in range(N_HEADS)], axis=0)
    o_ref[...] = jnp.transpose(o_t)


def _attn_prompt(qt, st, kb, vt, cst):
    b, nb, blk, a = kb.shape
    assert blk == MOBA_BLOCK and nb <= AUG_ROWS and a == ATTN_WIDTH and st.shape[1] == N_HEADS * AUG_ROWS
    return pl.pallas_call(
        _attn_prompt_kernel,
        grid=(b, nb),
        in_specs=[pl.BlockSpec(cst.shape, lambda bi, i: (0, 0, 0)),
                  pl.BlockSpec((None, a, blk), lambda bi, i: (bi, 0, i)),
                  pl.BlockSpec((None, N_HEADS * AUG_ROWS, blk), lambda bi, i: (bi, 0, i)),
                  pl.BlockSpec((None, nb, blk, a), lambda bi, i: (bi, 0, 0, 0)),
                  pl.BlockSpec((None, nb, a, blk), lambda bi, i: (bi, 0, 0, 0))],
        out_specs=pl.BlockSpec((None, blk, a), lambda bi, i: (bi, i, 0)),
        out_shape=jax.ShapeDtypeStruct((b, nb * blk, a), F32),
        scratch_shapes=[pltpu.VMEM((N_HEADS, 2 * LANE, blk), BF16),
                        pltpu.VMEM((N_HEADS, blk, blk), F32)],
        compiler_params=pltpu.CompilerParams(
            dimension_semantics=("arbitrary", "arbitrary"), vmem_limit_bytes=VMEM_LIMIT),
        name="attn_prompt",
    )(cst, qt, st, kb, vt)


def _ffn_parts(x_ref, a_ref, cn_ref, gao_ref, wo_ref, gffn_ref, wg_ref, wu_ref, wd_ref, gfin_ref,
               y_ref):
    ff = wg_ref.shape[1]
    split = (ff // 2 + 2 * LANE - 1) // (2 * LANE) * (2 * LANE)
    yield
    an = _rms(a_ref[...], gao_ref[...]).astype(BF16)
    mixed = jnp.concatenate([an, cn_ref[...]], axis=1)
    x1 = x_ref[...] + jnp.dot(mixed, wo_ref[...], preferred_element_type=F32)
    h2 = _rms(x1, gffn_ref[...]).astype(BF16)
    x2 = x1
    for lo, hi in ((0, split), (split, ff)):
        yield
        gate = jnp.dot(h2, wg_ref[:, lo:hi], preferred_element_type=F32)
        yield
        up = jnp.dot(h2, wu_ref[:, lo:hi], preferred_element_type=F32)
        act = (gate * (1.0 / (1.0 + jnp.exp(-gate))) * up).astype(BF16)
        yield
        x2 = x2 + jnp.dot(act, wd_ref[lo:hi, :], preferred_element_type=F32)
    yield
    y_ref[...] = _rms(x2, gfin_ref[...])


N_FFN_PARTS = 8
N_RING = 32


def _gate_stream(b, pt_ref, q_ref, ck_ref, idx_ref, buf_ref, sem_ref, part_ref):
    n_seq, n_pages = pt_ref.shape
    per_slice = n_pages // N_FFN_PARTS
    nblk = n_pages // PAGES_PER_BLOCK

    def page_copy(seq, page):
        return pltpu.make_async_copy(ck_ref.at[0, pt_ref[seq, page]], buf_ref.at[page % N_RING],
                                     sem_ref.at[page % N_RING])

    @pl.when(b == 0)
    def _():
        for page in range(N_RING):
            page_copy(0, page).start()

    for k in range(N_FFN_PARTS):
        pages = range(k * per_slice, (k + 1) * per_slice)
        for page in pages:
            page_copy(b, page).wait()
        yield
        for h in range(N_HEADS):
            qh = q_ref[h]
            for n in range(pages.start // PAGES_PER_BLOCK, pages.stop // PAGES_PER_BLOCK):
                acc = None
                for half in range(PAGES_PER_BLOCK):
                    prod = buf_ref[(n * PAGES_PER_BLOCK + half) % N_RING, h] * qh
                    red = jnp.sum(prod.reshape(HEAD_DIM // SUBLANE, SUBLANE, PAGE_SIZE), axis=0)
                    acc = red if acc is None else acc + red
                part_ref[n, h] = acc
        if pages.stop + N_RING <= n_pages:
            for page in pages:
                page_copy(b, page + N_RING).start()
        else:

            @pl.when(b + 1 < n_seq)
            def _():
                for page in pages:
                    page_copy(b + 1, page + N_RING - n_pages).start()

    g = jnp.sum(part_ref[...], axis=(2, 3))
    nidx = lax.broadcasted_iota(jnp.int32, g.shape, 0)
    picks = []
    for _ in range(MOBA_TOPK):
        mx = jnp.max(g, axis=0, keepdims=True)
        pick = jnp.min(jnp.where(g == mx, nidx, nblk), axis=0, keepdims=True)
        picks.append(pick)
        g = jnp.where(nidx == pick, -jnp.inf, g)
    idx_ref[...] = jnp.concatenate(picks, axis=0)


def _ffn_kernel(*refs, with_gate):
    if with_gate:
        pt_ref, q_ref, ck_ref = refs[0], refs[11], refs[12]
        ffn_refs = refs[1:11] + refs[13:14]
        idx_ref, buf_ref, sem_ref, part_ref = refs[14:18]
        stream = _gate_stream(pl.program_id(0), pt_ref, q_ref, ck_ref, idx_ref, buf_ref, sem_ref,
                              part_ref)
    else:
        ffn_refs = refs
        stream = iter(())
    for _ in _ffn_parts(*ffn_refs):
        next(stream, None)
    for _ in stream:
        pass


def _ffn(x, attn, cn, gao, wo, gffn, wg, wu, wd, gfin, tm, gate_inputs=None):
    n, d = x.shape
    a = attn.shape[1]
    c = cn.shape[1]
    assert n % tm == 0
    row = lambda w: pl.BlockSpec((tm, w), lambda i, *_: (i, 0))
    res = lambda arr: pl.BlockSpec(arr.shape, lambda i, *_: (0, 0), pipeline_mode=pl.Buffered(1))
    in_specs = [row(d), row(a), row(c), res(gao), res(wo), res(gffn), res(wg), res(wu), res(wd),
                res(gfin)]
    out_specs = [row(d)]
    out_shape = [jax.ShapeDtypeStruct((n, d), F32)]
    scratch = []
    args = (x, attn, cn, gao, wo, gffn, wg, wu, wd, gfin)
    if gate_inputs is not None:
        page_table, q_bc, cache_kt = gate_inputs
        n_seq, n_pages = page_table.shape
        nblk = n_pages // PAGES_PER_BLOCK
        per_slice = n_pages // N_FFN_PARTS
        assert n // tm == n_seq and n_pages % N_FFN_PARTS == 0 and per_slice % PAGES_PER_BLOCK == 0
        assert N_RING % per_slice == 0 and n_pages % N_RING == 0 and nblk >= MOBA_TOPK
        assert cache_kt.shape[2:] == (N_HEADS, HEAD_DIM, PAGE_SIZE) and HEAD_DIM % SUBLANE == 0
        in_specs += [pl.BlockSpec((None, N_HEADS, HEAD_DIM, PAGE_SIZE), lambda i, *_: (i, 0, 0, 0)),
                     pl.BlockSpec(memory_space=pl.ANY)]
        out_specs += [pl.BlockSpec((None, MOBA_TOPK, N_HEADS), lambda i, *_: (i, 0, 0))]
        out_shape += [jax.ShapeDtypeStruct((n_seq, MOBA_TOPK, N_HEADS), jnp.int32)]
        scratch = [pltpu.VMEM((N_RING, N_HEADS, HEAD_DIM, PAGE_SIZE), F32),
                   pltpu.SemaphoreType.DMA((N_RING,)),
                   pltpu.VMEM((nblk, N_HEADS, SUBLANE, PAGE_SIZE), F32)]
        args = (page_table,) + args + (q_bc, cache_kt)
    grid_spec = pltpu.PrefetchScalarGridSpec(
        num_scalar_prefetch=0 if gate_inputs is None else 1,
        grid=(n // tm,), in_specs=in_specs, out_specs=out_specs, scratch_shapes=scratch)
    out = pl.pallas_call(
        functools.partial(_ffn_kernel, with_gate=gate_inputs is not None),
        grid_spec=grid_spec,
        out_shape=out_shape,
        compiler_params=pltpu.CompilerParams(dimension_semantics=("arbitrary",),
                                             vmem_limit_bytes=VMEM_LIMIT),
        name="ffn_%d" % n,
    )(*args)
    return out[0] if gate_inputs is None else out


def _mix_sample_kernel(x_ref, g_ref, w_ref, wc_ref, gco_ref, h0_ref, h1_ref,
                       q_ref, k_ref, v_ref, cn_ref, xc_ref):
    a = ATTN_WIDTH
    c = cn_ref.shape[1]
    h = _rms(x_ref[...], g_ref[...]).astype(BF16)
    z = jnp.dot(h, w_ref[...], preferred_element_type=F32)
    q_ref[...] = z[:, 0:a]
    k_ref[...] = z[:, a:2 * a]
    v_ref[...] = z[:, 2 * a:3 * a]
    gb = z[:, 3 * a:3 * a + c]
    xc = z[:, 3 * a + c:3 * a + 2 * c] * z[:, 3 * a + 2 * c:3 * a + 3 * c]
    y = wc_ref[0:1, :] * h0_ref[...] + wc_ref[1:2, :] * h1_ref[...] + wc_ref[2:3, :] * xc
    cn_ref[...] = _rms(gb * y, gco_ref[...]).astype(BF16)
    xc_ref[...] = xc


def _mix_sample(x, g, w_bf, wc, gco, h0, h1):
    n = x.shape[0]
    a = ATTN_WIDTH
    c = wc.shape[1]
    return pl.pallas_call(
        _mix_sample_kernel,
        out_shape=[jax.ShapeDtypeStruct((n, a), F32)] * 3
        + [jax.ShapeDtypeStruct((n, c), BF16), jax.ShapeDtypeStruct((n, c), F32)],
        compiler_params=pltpu.CompilerParams(vmem_limit_bytes=VMEM_LIMIT),
        name="mix_sample",
    )(x, g, w_bf, wc, gco, h0, h1)


def _attn_sample_kernel(pt_ref, idx_ref, q_ref, kn_ref, vn_ref, ck_ref, cv_ref, o_ref,
                        kbuf_ref, vbuf_ref, sem_ref, *, past_len, slopes):
    b = pl.program_id(0)
    n_seq = pl.num_programs(0)
    n_slab = MOBA_TOPK * PAGES_PER_BLOCK
    slot = b % 2

    def copies(seq, sl):
        cps = []
        for h in range(N_HEADS):
            for t in range(MOBA_TOPK):
                blk = idx_ref[seq, t, h]
                for half in range(PAGES_PER_BLOCK):
                    pg = pt_ref[seq, blk * PAGES_PER_BLOCK + half]
                    u = t * PAGES_PER_BLOCK + half
                    cps.append(pltpu.make_async_copy(ck_ref.at[0, pg, h], kbuf_ref.at[sl, h, u],
                                                     sem_ref.at[0, sl]))
                    cps.append(pltpu.make_async_copy(cv_ref.at[0, pg, h], vbuf_ref.at[sl, h, u],
                                                     sem_ref.at[1, sl]))
        return cps

    @pl.when(b == 0)
    def _():
        for cp in copies(0, 0):
            cp.start()

    @pl.when(b + 1 < n_seq)
    def _():
        for cp in copies(b + 1, 1 - slot):
            cp.start()

    for cp in copies(b, slot):
        cp.wait()

    pos = lax.broadcasted_iota(jnp.int32, (1, PAGE_SIZE), 1)
    for h in range(N_HEADS):
        qh = q_ref[h] * ATTN_SCALE
        own = jnp.sum(qh * kn_ref[h], axis=0, keepdims=True)
        rows = []
        for t in range(MOBA_TOPK):
            blk = idx_ref[b, t, h]
            for half in range(PAGES_PER_BLOCK):
                u = t * PAGES_PER_BLOCK + half
                kpos = blk * MOBA_BLOCK + half * PAGE_SIZE + pos
                dist = (past_len - kpos).astype(F32)
                rows.append(jnp.sum(kbuf_ref[slot, h, u] * qh, axis=0, keepdims=True)
                            - float(slopes[h]) * dist)
        mrow = functools.reduce(jnp.maximum, rows)
        m = jnp.maximum(jnp.max(mrow, axis=1, keepdims=True), own)
        p_own = jnp.exp(own - m)
        den = p_own
        acc = jnp.zeros((HEAD_DIM, PAGE_SIZE), F32)
        for u in range(n_slab):
            p = jnp.exp(rows[u] - m)
            den = den + jnp.sum(p, axis=1, keepdims=True)
            acc = acc + vbuf_ref[slot, h, u] * p
        num = jnp.sum(acc, axis=1, keepdims=True) + (p_own * vn_ref[h])[:, 0:1]
        o_ref[:, h:h + 1] = num / den[:, 0:1]


def _attn_sample(page_table, idx, q_bc, kn_bc, vn_bc, cache_kt, cache_vt, past_len):
    n_seq = q_bc.shape[0]
    blk4 = pl.BlockSpec((None, N_HEADS, HEAD_DIM, PAGE_SIZE), lambda i, pt, ix: (i, 0, 0, 0))
    n_slab = MOBA_TOPK * PAGES_PER_BLOCK
    grid_spec = pltpu.PrefetchScalarGridSpec(
        num_scalar_prefetch=2,
        grid=(n_seq,),
        in_specs=[blk4, blk4, blk4, pl.BlockSpec(memory_space=pl.ANY), pl.BlockSpec(memory_space=pl.ANY)],
        out_specs=pl.BlockSpec((None, HEAD_DIM, N_HEADS), lambda i, pt, ix: (i, 0, 0)),
        scratch_shapes=[pltpu.VMEM((2, N_HEADS, n_slab, HEAD_DIM, PAGE_SIZE), F32),
                        pltpu.VMEM((2, N_HEADS, n_slab, HEAD_DIM, PAGE_SIZE), F32),
                        pltpu.SemaphoreType.DMA((2, 2))])
    kern = functools.partial(_attn_sample_kernel, past_len=past_len, slopes=_alibi_slopes_np())
    return pl.pallas_call(
        kern,
        grid_spec=grid_spec,
        out_shape=jax.ShapeDtypeStruct((n_seq, HEAD_DIM, N_HEADS), F32),
        compiler_params=pltpu.CompilerParams(dimension_semantics=("arbitrary",),
                                             vmem_limit_bytes=VMEM_LIMIT),
        name="attn_sample",
    )(page_table, idx, q_bc, kn_bc, vn_bc, cache_kt, cache_vt)


def kernel(x_prompt, x_sample, cache_k, cache_v, state_conv, page_table, norm_mix, w_mix, w_conv,
           norm_attn_out, norm_conv_out, w_o, norm_ffn, w_gate, w_up, w_down, norm_final):
    depth = w_mix.shape[0]
    assert depth == 1 and x_sample.shape[1] == 1
    b, s, d = x_prompt.shape
    n_seq = x_sample.shape[0]
    n_pages = page_table.shape[1]
    past_len = n_pages * PAGE_SIZE
    a = ATTN_WIDTH
    c = w_conv.shape[2]

    g_mix = norm_mix[0][None, :]
    g_ao = norm_attn_out[0][None, :]
    g_co = norm_conv_out[0][None, :]
    g_ffn = norm_ffn[0][None, :]
    g_fin = norm_final[None, :]
    w_mix_bf = w_mix[0].astype(BF16)
    w_o_bf = w_o[0].astype(BF16)
    w_gate_bf = w_gate[0].astype(BF16)
    w_up_bf = w_up[0].astype(BF16)
    w_down_bf = w_down[0].astype(BF16)
    wc = w_conv[0]

    sl = (_alibi_slopes_np().astype(np.float64) * LOG2E).astype(np.float32)
    hi = sl.astype(jnp.bfloat16).astype(np.float32)
    mid = (sl - hi).astype(jnp.bfloat16).astype(np.float32)
    lo = (sl - hi - mid).astype(jnp.bfloat16).astype(np.float32)
    cst_np = np.zeros((N_HEADS, AUG_ROWS, MOBA_BLOCK), np.float32)
    for r, piece in enumerate((hi, mid, lo, hi, mid, lo)):
        cst_np[:, r, :] = piece[:, None]
    cst = jnp.asarray(cst_np, dtype=BF16)

    xs = x_sample[:, 0, :]
    h0 = state_conv[0, :, 0, :]
    h1 = state_conv[0, :, 1, :]
    q_s, k_s, v_s, cn_s, xc_s = _mix_sample(xs, g_mix, w_mix_bf, wc, g_co, h0, h1)
    hs = (N_HEADS, HEAD_DIM)
    lane_bc = lambda t: jnp.broadcast_to(t.reshape((n_seq,) + hs + (1,)), (n_seq,) + hs + (PAGE_SIZE,))
    q_bc, kn_bc, vn_bc = lane_bc(q_s), lane_bc(k_s), lane_bc(v_s)
    cache_kt = jnp.transpose(cache_k, (0, 1, 3, 4, 2))
    cache_vt = jnp.transpose(cache_v, (0, 1, 3, 4, 2))

    k_p, v_p, kb_p, vt_p, qt_p, st_p, cn_p, cst_p = _mix_prompt(x_prompt, g_mix, w_mix_bf, wc, g_co)
    attn_p = _attn_prompt(qt_p, st_p, kb_p.reshape(b, s // MOBA_BLOCK, MOBA_BLOCK, a), vt_p, cst)
    y_p, idx = _ffn(x_prompt.reshape(b * s, d), attn_p.reshape(b * s, a), cn_p.reshape(b * s, c),
                    g_ao, w_o_bf, g_ffn, w_gate_bf, w_up_bf, w_down_bf, g_fin, tm=b * s // n_seq,
                    gate_inputs=(page_table, q_bc, cache_kt))
    y_p = y_p.reshape(b, s, d)

    attn_s = _attn_sample(page_table, idx, q_bc, kn_bc, vn_bc, cache_kt, cache_vt, past_len)
    attn_s = jnp.transpose(attn_s, (0, 2, 1)).reshape(n_seq, a)
    y_s = _ffn(xs, attn_s, cn_s, g_ao, w_o_bf, g_ffn, w_gate_bf, w_up_bf, w_down_bf, g_fin, tm=n_seq)

    return (y_p, y_s[:, None, :],
            k_p.reshape((1, b, s) + hs), v_p.reshape((1, b, s) + hs), cst_p[None],
            k_s.reshape((1, n_seq, 1) + hs), v_s.reshape((1, n_seq, 1) + hs),
            jnp.stack([h1, xc_s], axis=1)[None])
```

```python
import functools

import numpy as np
import jax
import jax.numpy as jnp
from jax import lax
from jax.experimental import pallas as pl
from jax.experimental.pallas import tpu as pltpu

F32 = jnp.float32
BF16 = jnp.bfloat16

HEAD_DIM = 64
N_HEADS = 8
ATTN_WIDTH = N_HEADS * HEAD_DIM
MOBA_BLOCK = 256
MOBA_TOPK = 3
PAGE_SIZE = 128
PAGES_PER_BLOCK = MOBA_BLOCK // PAGE_SIZE
RMS_EPS = 1e-6
ATTN_SCALE = HEAD_DIM ** -0.5
LOG2E = 1.4426950408889634
NEG = -1e30
LANE = 128
SUBLANE = 8
VMEM_LIMIT = 56 * 1024 * 1024


def _alibi_slopes_np():
    return (2.0 ** (-8.0 * np.arange(1, N_HEADS + 1, dtype=np.float64) / N_HEADS)).astype(np.float32)


def _rms(x, g):
    y = x * lax.rsqrt(jnp.mean(x * x, axis=-1, keepdims=True) + RMS_EPS)
    return y * g


def _split_bf16(x):
    hi = x.astype(BF16)
    lo = (x - hi.astype(F32)).astype(BF16)
    return hi, lo


_DN_T = (((1,), (1,)), ((), ()))


MIX_CHUNK = 512


def _mix_project(x_ref, g_ref, w_ref, z_ref, finish=()):
    h = _rms(x_ref[...], g_ref[...]).astype(BF16)
    finish = iter(finish)
    for c0 in range(0, w_ref.shape[1], MIX_CHUNK):
        z_ref[:, c0:c0 + MIX_CHUNK] = jnp.dot(h, w_ref[:, c0:c0 + MIX_CHUNK],
                                              preferred_element_type=F32)
        next(finish, None)
    for _ in finish:
        pass


def _mix_finish(s, z_ref, wc_ref, gco_ref,
                k_ref, v_ref, kb_ref, vt_ref, qt_ref, st_ref, cn_ref, cst_ref, carry_ref, kmean_ref):
    tm = z_ref.shape[0]
    a = ATTN_WIDTH
    c = cn_ref.shape[1]
    first = s == 0

    k = z_ref[:, a:2 * a]
    v = z_ref[:, 2 * a:3 * a]
    k_ref[...] = k
    v_ref[...] = v
    kb_ref[...] = k.astype(BF16)
    vt = jnp.transpose(v)
    vt_ref[...] = vt.astype(BF16)
    ksum = jnp.sum(k, axis=0, keepdims=True) * (1.0 / MOBA_BLOCK)
    yield

    q = z_ref[:, 0:a]
    qt = jnp.transpose(q * (ATTN_SCALE * LOG2E))
    qt_ref[...] = qt.astype(BF16)
    yield

    gb = z_ref[:, 3 * a:3 * a + c]
    gc = z_ref[:, 3 * a + c:3 * a + 2 * c]
    u = z_ref[:, 3 * a + 2 * c:3 * a + 3 * c]
    xc = gc * u
    row = lax.broadcasted_iota(jnp.int32, xc.shape, 0)
    p0 = jnp.where(first, 0.0, carry_ref[0:1, :])
    p1 = jnp.where(first, 0.0, carry_ref[1:2, :])
    xc1 = jnp.where(row == 0, p1, pltpu.roll(xc, 1, axis=0))
    xc2 = jnp.where(row == 0, p0, jnp.where(row == 1, p1, pltpu.roll(xc, 2, axis=0)))
    y = wc_ref[0:1, :] * xc2 + wc_ref[1:2, :] * xc1 + wc_ref[2:3, :] * xc
    cn = _rms(gb * y, gco_ref[...])
    cn_ref[...] = cn.astype(BF16)
    last2 = xc[tm - 2:tm, :]
    carry_ref[...] = last2
    cst_ref[...] = last2
    yield

    nb = kmean_ref.shape[0]
    krow = lax.broadcasted_iota(jnp.int32, kmean_ref.shape, 0)
    km = jnp.where(krow < s, kmean_ref[...], 0.0)
    kmt = jnp.concatenate([km] * N_HEADS, axis=0)
    rr = lax.broadcasted_iota(jnp.int32, kmt.shape, 0) // nb
    cc = lax.broadcasted_iota(jnp.int32, kmt.shape, 1) // HEAD_DIM
    kmt = jnp.where(rr == cc, kmt, 0.0)
    qh, ql = _split_bf16(q)
    kh, kl = _split_bf16(kmt)
    gate = (lax.dot_general(qh, kh, _DN_T, preferred_element_type=F32)
            + lax.dot_general(ql, kh, _DN_T, preferred_element_type=F32)
            + lax.dot_general(qh, kl, _DN_T, preferred_element_type=F32))

    lane = lax.broadcasted_iota(jnp.int32, gate.shape, 1)
    jl = lane % nb
    valid = jl < s
    gm = jnp.where(valid, gate, -jnp.inf)
    cnt = jnp.zeros(gate.shape, F32)
    width = gate.shape[1]
    yield
    for sh in range(1, nb):
        fwd = pltpu.roll(gm, width - sh, axis=1)
        bwd = pltpu.roll(gm, nb - sh, axis=1)
        wrapped = (jl + sh) >= nb
        beats = jnp.where(wrapped, jnp.where(bwd >= gm, 1.0, 0.0), jnp.where(fwd > gm, 1.0, 0.0))
        cnt = cnt + beats
        if sh % 8 == 0:
            yield
    selb = jnp.where(valid, jnp.where(cnt < MOBA_TOPK, 0.0, NEG), NEG)
    selb = jnp.where(jl == s, 0.0, selb)
    st_ref[...] = jnp.transpose(selb).astype(BF16)

    kmean_ref[pl.ds(s, 1), :] = ksum


def _mix_prompt_kernel(x_ref, g_ref, w_ref, wc_ref, gco_ref,
                       k_ref, v_ref, kb_ref, vt_ref, qt_ref, st_ref, cn_ref, cst_ref,
                       carry_ref, kmean_ref, za_ref, zb_ref, *, tiles_per_seq):
    t = pl.program_id(0)
    s_prev = (t + tiles_per_seq - 1) % tiles_per_seq
    outs = (k_ref, v_ref, kb_ref, vt_ref, qt_ref, st_ref, cn_ref, cst_ref, carry_ref, kmean_ref)

    @pl.when(t == 0)
    def _():
        carry_ref[...] = jnp.zeros_like(carry_ref)
        kmean_ref[...] = jnp.zeros_like(kmean_ref)
        _mix_project(x_ref, g_ref, w_ref, za_ref)

    @pl.when(t % 2 == 1)
    def _():
        _mix_project(x_ref, g_ref, w_ref, zb_ref, _mix_finish(s_prev, za_ref, wc_ref, gco_ref, *outs))

    @pl.when((t % 2 == 0) & (t > 0))
    def _():
        _mix_project(x_ref, g_ref, w_ref, za_ref, _mix_finish(s_prev, zb_ref, wc_ref, gco_ref, *outs))


def _mix_prompt(x, g, w_bf, wc, gco):
    b, s, d = x.shape
    tm = MOBA_BLOCK
    ns = s // tm
    nt = b * ns
    a = ATTN_WIDTH
    c = wc.shape[1]
    assert s % tm == 0 and ns * N_HEADS == LANE and w_bf.shape[1] == 3 * a + 3 * c and nt % 2 == 0
    tin = lambda t: jnp.minimum(t, nt - 1)
    tout = lambda t: jnp.maximum(t - 1, 0)
    row_in = lambda n: pl.BlockSpec((None, tm, n), lambda t: (tin(t) // ns, tin(t) % ns, 0))
    row_spec = lambda n: pl.BlockSpec((None, tm, n), lambda t: (tout(t) // ns, tout(t) % ns, 0))
    col_spec = lambda n: pl.BlockSpec((None, n, tm), lambda t: (tout(t) // ns, 0, tout(t) % ns))
    full = lambda shp: pl.BlockSpec(shp, lambda t: (0,) * len(shp))
    return pl.pallas_call(
        functools.partial(_mix_prompt_kernel, tiles_per_seq=ns),
        grid=(nt + 1,),
        in_specs=[row_in(d), full((1, d)), full(w_bf.shape), full(wc.shape), full((1, c))],
        out_specs=[row_spec(a), row_spec(a), row_spec(a),
                   pl.BlockSpec((None, None, a, tm), lambda t: (tout(t) // ns, tout(t) % ns, 0, 0)),
                   col_spec(a), col_spec(LANE), row_spec(c),
                   pl.BlockSpec((None, 2, c), lambda t: (tout(t) // ns, 0, 0))],
        out_shape=[jax.ShapeDtypeStruct((b, s, a), F32),
                   jax.ShapeDtypeStruct((b, s, a), F32),
                   jax.ShapeDtypeStruct((b, s, a), BF16),
                   jax.ShapeDtypeStruct((b, ns, a, tm), BF16),
                   jax.ShapeDtypeStruct((b, a, s), BF16),
                   jax.ShapeDtypeStruct((b, LANE, s), BF16),
                   jax.ShapeDtypeStruct((b, s, c), BF16),
                   jax.ShapeDtypeStruct((b, 2, c), F32)],
        scratch_shapes=[pltpu.VMEM((2, c), F32), pltpu.VMEM((ns, a), F32),
                        pltpu.VMEM((tm, w_bf.shape[1]), F32), pltpu.VMEM((tm, w_bf.shape[1]), F32)],
        compiler_params=pltpu.CompilerParams(dimension_semantics=("arbitrary",),
                                             vmem_limit_bytes=VMEM_LIMIT),
        name="mix_prompt",
    )(x, g, w_bf, wc, gco)


AUG_ROWS = 16


LOOKAHEAD = N_HEADS


def _attn_prompt_kernel(cst_ref, qt_ref, st_ref, kb_ref, vt_ref, o_ref, w_ref, sc_ref, *state_refs):
    i = pl.program_id(1)
    blk = MOBA_BLOCK
    hd = HEAD_DIM

    zq = jnp.zeros((hd, blk), BF16)
    zpad = jnp.zeros((LANE - 2 * AUG_ROWS, blk), BF16)
    for h in range(N_HEADS):
        qh = qt_ref[h * hd:(h + 1) * hd, :]
        qpart = [qh, zq] if h % 2 == 0 else [zq, qh]
        w_ref[h] = jnp.concatenate(
            qpart + [st_ref[h * AUG_ROWS:(h + 1) * AUG_ROWS, :], cst_ref[h], zpad], axis=0)

    elane = lax.broadcasted_iota(jnp.int32, (blk, LANE), 1)
    erow = lax.broadcasted_iota(jnp.int32, (blk, LANE), 0).astype(F32)
    ebase = jnp.where((elane >= AUG_ROWS) & (elane < AUG_ROWS + 3), erow, 0.0)
    rk = lax.broadcasted_iota(jnp.int32, (blk, blk), 0)
    rq = lax.broadcasted_iota(jnp.int32, (blk, blk), 1)
    ones_rows = jnp.ones((2 * SUBLANE, blk), BF16)

    def key_aug(j):
        off = ((j - i) * blk).astype(F32)
        return jnp.where(elane == j, 1.0,
                         jnp.where((elane >= AUG_ROWS + 3) & (elane < AUG_ROWS + 6), off, ebase)).astype(BF16)

    def scores(kbj, e, h):
        rhs = jnp.concatenate([kbj[:, (h // 2) * LANE:(h // 2 + 1) * LANE], e], axis=1)
        return jnp.dot(rhs, w_ref[h], preferred_element_type=F32)

    def block(j, nxt, diagonal):
        kbj = kb_ref[j]
        vtj = vt_ref[j]
        e = key_aug(j)
        if nxt is not None:
            kbn = kb_ref[nxt]
            en = key_aug(nxt)
        for h in range(N_HEADS):
            st = sc_ref[h]
            ahead = h + LOOKAHEAD
            if ahead < N_HEADS:
                sc_ref[ahead] = scores(kbj, e, ahead)
            elif nxt is not None:
                sc_ref[ahead - N_HEADS] = scores(kbn, en, ahead - N_HEADS)
            if diagonal:
                st = st + jnp.where(rk <= rq, 0.0, NEG)
            acc_ref, m_ref, l_ref = state_refs[h], state_refs[N_HEADS + h], state_refs[2 * N_HEADS + h]
            m_old = m_ref[...]
            m_new = jnp.maximum(m_old, jnp.max(st, axis=0, keepdims=True))
            alpha = jnp.exp2(m_old - m_new)
            p = jnp.exp2(st - m_new)
            m_ref[...] = m_new
            pv = jnp.dot(jnp.concatenate([vtj[h * hd:(h + 1) * hd, :], ones_rows], axis=0),
                         p.astype(BF16), preferred_element_type=F32)
            l_ref[...] = alpha * l_ref[...] + pv[hd:hd + 1, :]
            acc_ref[...] = alpha * acc_ref[...] + pv[0:hd, :]

    kb0 = kb_ref[0]
    e0 = key_aug(0)
    for h in range(LOOKAHEAD):
        sc_ref[h] = scores(kb0, e0, h)
    for h in range(N_HEADS):
        state_refs[h][...] = jnp.zeros((hd, blk), F32)
        state_refs[N_HEADS + h][...] = jnp.full((1, blk), -jnp.inf, F32)
        state_refs[2 * N_HEADS + h][...] = jnp.zeros((1, blk), F32)

    def body(j, carry):
        block(j, j + 1, False)
        return carry

    lax.fori_loop(0, i, body, 0)
    block(i, None, True)
    o_t = jnp.concatenate([state_refs[h][...] / state_refs[2 * N_HEADS + h][...] for h in range(N_HEADS)],
                          axis=0)
    o_ref[...] = jnp.transpose(o_t)


def _attn_prompt(qt, st, kb, vt, cst):
    b, nb, blk, a = kb.shape
    assert blk == MOBA_BLOCK and nb <= AUG_ROWS and a == ATTN_WIDTH and st.shape[1] == N_HEADS * AUG_ROWS
    return pl.pallas_call(
        _attn_prompt_kernel,
        grid=(b, nb),
        in_specs=[pl.BlockSpec(cst.shape, lambda bi, i: (0, 0, 0)),
                  pl.BlockSpec((None, a, blk), lambda bi, i: (bi, 0, i)),
                  pl.BlockSpec((None, N_HEADS * AUG_ROWS, blk), lambda bi, i: (bi, 0, i)),
                  pl.BlockSpec((None, nb, blk, a), lambda bi, i: (bi, 0, 0, 0)),
                  pl.BlockSpec((None, nb, a, blk), lambda bi, i: (bi, 0, 0, 0))],
        out_specs=pl.BlockSpec((None, blk, a), lambda bi, i: (bi, i, 0)),
        out_shape=jax.ShapeDtypeStruct((b, nb * blk, a), F32),
        scratch_shapes=([pltpu.VMEM((N_HEADS, 2 * LANE, blk), BF16),
                         pltpu.VMEM((N_HEADS, blk, blk), F32)]
                        + [pltpu.VMEM((HEAD_DIM, blk), F32)] * N_HEADS
                        + [pltpu.VMEM((1, blk), F32)] * (2 * N_HEADS)),
        compiler_params=pltpu.CompilerParams(
            dimension_semantics=("arbitrary", "arbitrary"), vmem_limit_bytes=VMEM_LIMIT),
        name="attn_prompt",
    )(cst, qt, st, kb, vt)


class _NoSide:
    def region(self):
        pass

    def piece(self):
        pass

    def close(self):
        pass


MXU_TILE = 2 * LANE


def _ffn_body(x_ref, a_ref, cn_ref, gao_ref, wo_ref, gffn_ref, wg_ref, wu_ref, wd_ref, gfin_ref,
              y_ref, side):
    ff = wg_ref.shape[1]
    split = (ff // 2 + MXU_TILE - 1) // MXU_TILE * MXU_TILE

    def mm(lhs, w_ref, rows, cols):
        tiles = []
        for c0 in range(cols.start, cols.stop, MXU_TILE):
            c1 = min(c0 + MXU_TILE, cols.stop)
            tiles.append(jnp.dot(lhs, w_ref[rows, c0:c1], preferred_element_type=F32))
            side.piece()
        return jnp.concatenate(tiles, axis=1)

    d = x_ref.shape[1]
    side.region()
    an = _rms(a_ref[...], gao_ref[...]).astype(BF16)
    mixed = jnp.concatenate([an, cn_ref[...]], axis=1)
    x1 = x_ref[...] + mm(mixed, wo_ref, slice(None), range(0, d))
    h2 = _rms(x1, gffn_ref[...]).astype(BF16)
    x2 = x1
    for lo, hi in ((0, split), (split, ff)):
        side.region()
        gate = mm(h2, wg_ref, slice(None), range(lo, hi))
        side.region()
        up = mm(h2, wu_ref, slice(None), range(lo, hi))
        act = (gate * (1.0 / (1.0 + jnp.exp(-gate))) * up).astype(BF16)
        side.region()
        x2 = x2 + mm(act, wd_ref, slice(lo, hi), range(0, d))
    side.region()
    y_ref[...] = _rms(x2, gfin_ref[...])
    side.close()


N_FFN_PARTS = 8
N_RING = 32
BLOCKS_PER_PIECE = 2


class _GateStream:
    def __init__(self, b, pt_ref, q_ref, ck_ref, idx_ref, buf_ref, sem_ref, part_ref):
        self.b, self.pt_ref, self.q_ref, self.ck_ref = b, pt_ref, q_ref, ck_ref
        self.idx_ref, self.buf_ref, self.sem_ref, self.part_ref = idx_ref, buf_ref, sem_ref, part_ref
        self.n_seq, self.n_pages = pt_ref.shape
        self.per_slice = self.n_pages // N_FFN_PARTS
        self.k = -1
        self.pending = []

        @pl.when(b == 0)
        def _():
            for page in range(N_RING):
                self._copy(0, page).start()

    def _copy(self, seq, page):
        return pltpu.make_async_copy(self.ck_ref.at[0, self.pt_ref[seq, page]],
                                     self.buf_ref.at[page % N_RING], self.sem_ref.at[page % N_RING])

    def _pages(self):
        return range(self.k * self.per_slice, (self.k + 1) * self.per_slice)

    def region(self):
        if self.k >= 0:
            self._finish_slice()
        self.k += 1
        if self.k < N_FFN_PARTS:
            pages = self._pages()
            for page in pages:
                self._copy(self.b, page).wait()
            blocks = list(range(pages.start // PAGES_PER_BLOCK, pages.stop // PAGES_PER_BLOCK))
            self.pending = [blocks[i:i + BLOCKS_PER_PIECE] for i in range(0, len(blocks), BLOCKS_PER_PIECE)]

    def piece(self):
        if not self.pending:
            return
        blocks = self.pending.pop(0)
        for h in range(N_HEADS):
            qh = self.q_ref[h]
            for n in blocks:
                ksum = None
                for half in range(PAGES_PER_BLOCK):
                    page = self.buf_ref[(n * PAGES_PER_BLOCK + half) % N_RING, h]
                    ksum = page if ksum is None else ksum + page
                prod = ksum * qh
                self.part_ref[n, h] = jnp.sum(
                    prod.reshape(HEAD_DIM // SUBLANE, SUBLANE, PAGE_SIZE), axis=0)

    def _finish_slice(self):
        while self.pending:
            self.piece()
        pages = self._pages()
        if pages.stop + N_RING <= self.n_pages:
            for page in pages:
                self._copy(self.b, page + N_RING).start()
        else:

            @pl.when(self.b + 1 < self.n_seq)
            def _():
                for page in pages:
                    self._copy(self.b + 1, page + N_RING - self.n_pages).start()

    def close(self):
        self.region()
        nblk = self.n_pages // PAGES_PER_BLOCK
        g = jnp.sum(self.part_ref[...], axis=(2, 3))
        nidx = lax.broadcasted_iota(jnp.int32, g.shape, 0)
        picks = []
        for _ in range(MOBA_TOPK):
            mx = jnp.max(g, axis=0, keepdims=True)
            pick = jnp.min(jnp.where(g == mx, nidx, nblk), axis=0, keepdims=True)
            picks.append(pick)
            g = jnp.where(nidx == pick, -jnp.inf, g)
        self.idx_ref[...] = jnp.concatenate(picks, axis=0)


def _ffn_kernel(*refs, with_gate):
    if with_gate:
        pt_ref, q_ref, ck_ref = refs[0], refs[11], refs[12]
        ffn_refs = refs[1:11] + refs[13:14]
        side = _GateStream(pl.program_id(0), pt_ref, q_ref, ck_ref, *refs[14:18])
    else:
        ffn_refs = refs
        side = _NoSide()
    _ffn_body(*ffn_refs, side)


def _ffn(x, attn, cn, gao, wo, gffn, wg, wu, wd, gfin, tm, gate_inputs=None):
    n, d = x.shape
    a = attn.shape[1]
    c = cn.shape[1]
    assert n % tm == 0
    row = lambda w: pl.BlockSpec((tm, w), lambda i, *_: (i, 0))
    res = lambda arr: pl.BlockSpec(arr.shape, lambda i, *_: (0, 0), pipeline_mode=pl.Buffered(1))
    in_specs = [row(d), row(a), row(c), res(gao), res(wo), res(gffn), res(wg), res(wu), res(wd),
                res(gfin)]
    out_specs = [row(d)]
    out_shape = [jax.ShapeDtypeStruct((n, d), F32)]
    scratch = []
    args = (x, attn, cn, gao, wo, gffn, wg, wu, wd, gfin)
    if gate_inputs is not None:
        page_table, q_bc, cache_kt = gate_inputs
        n_seq, n_pages = page_table.shape
        nblk = n_pages // PAGES_PER_BLOCK
        per_slice = n_pages // N_FFN_PARTS
        assert n // tm == n_seq and n_pages % N_FFN_PARTS == 0 and per_slice % PAGES_PER_BLOCK == 0
        assert N_RING % per_slice == 0 and n_pages % N_RING == 0 and nblk >= MOBA_TOPK
        assert cache_kt.shape[2:] == (N_HEADS, HEAD_DIM, PAGE_SIZE) and HEAD_DIM % SUBLANE == 0
        in_specs += [pl.BlockSpec((None, N_HEADS, HEAD_DIM, PAGE_SIZE), lambda i, *_: (i, 0, 0, 0)),
                     pl.BlockSpec(memory_space=pl.ANY)]
        out_specs += [pl.BlockSpec((None, MOBA_TOPK, N_HEADS), lambda i, *_: (i, 0, 0))]
        out_shape += [jax.ShapeDtypeStruct((n_seq, MOBA_TOPK, N_HEADS), jnp.int32)]
        scratch = [pltpu.VMEM((N_RING, N_HEADS, HEAD_DIM, PAGE_SIZE), F32),
                   pltpu.SemaphoreType.DMA((N_RING,)),
                   pltpu.VMEM((nblk, N_HEADS, SUBLANE, PAGE_SIZE), F32)]
        args = (page_table,) + args + (q_bc, cache_kt)
    grid_spec = pltpu.PrefetchScalarGridSpec(
        num_scalar_prefetch=0 if gate_inputs is None else 1,
        grid=(n // tm,), in_specs=in_specs, out_specs=out_specs, scratch_shapes=scratch)
    out = pl.pallas_call(
        functools.partial(_ffn_kernel, with_gate=gate_inputs is not None),
        grid_spec=grid_spec,
        out_shape=out_shape,
        compiler_params=pltpu.CompilerParams(dimension_semantics=("arbitrary",),
                                             vmem_limit_bytes=VMEM_LIMIT),
        name="ffn_%d" % n,
    )(*args)
    return out[0] if gate_inputs is None else out


def _mix_sample_kernel(x_ref, g_ref, w_ref, wc_ref, gco_ref, h0_ref, h1_ref,
                       q_ref, k_ref, v_ref, cn_ref, xc_ref):
    a = ATTN_WIDTH
    c = cn_ref.shape[1]
    h = _rms(x_ref[...], g_ref[...]).astype(BF16)
    z = jnp.dot(h, w_ref[...], preferred_element_type=F32)
    q_ref[...] = z[:, 0:a]
    k_ref[...] = z[:, a:2 * a]
    v_ref[...] = z[:, 2 * a:3 * a]
    gb = z[:, 3 * a:3 * a + c]
    xc = z[:, 3 * a + c:3 * a + 2 * c] * z[:, 3 * a + 2 * c:3 * a + 3 * c]
    y = wc_ref[0:1, :] * h0_ref[...] + wc_ref[1:2, :] * h1_ref[...] + wc_ref[2:3, :] * xc
    cn_ref[...] = _rms(gb * y, gco_ref[...]).astype(BF16)
    xc_ref[...] = xc


def _mix_sample(x, g, w_bf, wc, gco, h0, h1):
    n = x.shape[0]
    a = ATTN_WIDTH
    c = wc.shape[1]
    return pl.pallas_call(
        _mix_sample_kernel,
        out_shape=[jax.ShapeDtypeStruct((n, a), F32)] * 3
        + [jax.ShapeDtypeStruct((n, c), BF16), jax.ShapeDtypeStruct((n, c), F32)],
        compiler_params=pltpu.CompilerParams(vmem_limit_bytes=VMEM_LIMIT),
        name="mix_sample",
    )(x, g, w_bf, wc, gco, h0, h1)


def _attn_sample_kernel(pt_ref, idx_ref, q_ref, kn_ref, vn_ref, ck_ref, cv_ref, o_ref,
                        kbuf_ref, vbuf_ref, sem_ref, *, past_len, slopes):
    b = pl.program_id(0)
    n_seq = pl.num_programs(0)
    n_slab = MOBA_TOPK * PAGES_PER_BLOCK
    slot = b % 2

    def copies(seq, sl):
        cps = []
        for h in range(N_HEADS):
            for t in range(MOBA_TOPK):
                blk = idx_ref[seq, t, h]
                for half in range(PAGES_PER_BLOCK):
                    pg = pt_ref[seq, blk * PAGES_PER_BLOCK + half]
                    u = t * PAGES_PER_BLOCK + half
                    cps.append(pltpu.make_async_copy(ck_ref.at[0, pg, h], kbuf_ref.at[sl, h, u],
                                                     sem_ref.at[0, sl]))
                    cps.append(pltpu.make_async_copy(cv_ref.at[0, pg, h], vbuf_ref.at[sl, h, u],
                                                     sem_ref.at[1, sl]))
        return cps

    @pl.when(b == 0)
    def _():
        for cp in copies(0, 0):
            cp.start()

    @pl.when(b + 1 < n_seq)
    def _():
        for cp in copies(b + 1, 1 - slot):
            cp.start()

    for cp in copies(b, slot):
        cp.wait()

    pos = lax.broadcasted_iota(jnp.int32, (1, PAGE_SIZE), 1)
    for h in range(N_HEADS):
        qh = q_ref[h] * ATTN_SCALE
        own = jnp.sum(qh * kn_ref[h], axis=0, keepdims=True)
        rows = []
        for t in range(MOBA_TOPK):
            blk = idx_ref[b, t, h]
            for half in range(PAGES_PER_BLOCK):
                u = t * PAGES_PER_BLOCK + half
                kpos = blk * MOBA_BLOCK + half * PAGE_SIZE + pos
                dist = (past_len - kpos).astype(F32)
                rows.append(jnp.sum(kbuf_ref[slot, h, u] * qh, axis=0, keepdims=True)
                            - float(slopes[h]) * dist)
        mrow = functools.reduce(jnp.maximum, rows)
        m = jnp.maximum(jnp.max(mrow, axis=1, keepdims=True), own)
        p_own = jnp.exp(own - m)
        den = p_own
        acc = jnp.zeros((HEAD_DIM, PAGE_SIZE), F32)
        for u in range(n_slab):
            p = jnp.exp(rows[u] - m)
            den = den + jnp.sum(p, axis=1, keepdims=True)
            acc = acc + vbuf_ref[slot, h, u] * p
        num = jnp.sum(acc, axis=1, keepdims=True) + (p_own * vn_ref[h])[:, 0:1]
        o_ref[:, h:h + 1] = num / den[:, 0:1]


def _attn_sample(page_table, idx, q_bc, kn_bc, vn_bc, cache_kt, cache_vt, past_len):
    n_seq = q_bc.shape[0]
    blk4 = pl.BlockSpec((None, N_HEADS, HEAD_DIM, PAGE_SIZE), lambda i, pt, ix: (i, 0, 0, 0))
    n_slab = MOBA_TOPK * PAGES_PER_BLOCK
    grid_spec = pltpu.PrefetchScalarGridSpec(
        num_scalar_prefetch=2,
        grid=(n_seq,),
        in_specs=[blk4, blk4, blk4, pl.BlockSpec(memory_space=pl.ANY), pl.BlockSpec(memory_space=pl.ANY)],
        out_specs=pl.BlockSpec((None, HEAD_DIM, N_HEADS), lambda i, pt, ix: (i, 0, 0)),
        scratch_shapes=[pltpu.VMEM((2, N_HEADS, n_slab, HEAD_DIM, PAGE_SIZE), F32),
                        pltpu.VMEM((2, N_HEADS, n_slab, HEAD_DIM, PAGE_SIZE), F32),
                        pltpu.SemaphoreType.DMA((2, 2))])
    kern = functools.partial(_attn_sample_kernel, past_len=past_len, slopes=_alibi_slopes_np())
    return pl.pallas_call(
        kern,
        grid_spec=grid_spec,
        out_shape=jax.ShapeDtypeStruct((n_seq, HEAD_DIM, N_HEADS), F32),
        compiler_params=pltpu.CompilerParams(dimension_semantics=("arbitrary",),
                                             vmem_limit_bytes=VMEM_LIMIT),
        name="attn_sample",
    )(page_table, idx, q_bc, kn_bc, vn_bc, cache_kt, cache_vt)


def kernel(x_prompt, x_sample, cache_k, cache_v, state_conv, page_table, norm_mix, w_mix, w_conv,
           norm_attn_out, norm_conv_out, w_o, norm_ffn, w_gate, w_up, w_down, norm_final):
    depth = w_mix.shape[0]
    assert depth == 1 and x_sample.shape[1] == 1
    b, s, d = x_prompt.shape
    n_seq = x_sample.shape[0]
    n_pages = page_table.shape[1]
    past_len = n_pages * PAGE_SIZE
    a = ATTN_WIDTH
    c = w_conv.shape[2]

    g_mix = norm_mix[0][None, :]
    g_ao = norm_attn_out[0][None, :]
    g_co = norm_conv_out[0][None, :]
    g_ffn = norm_ffn[0][None, :]
    g_fin = norm_final[None, :]
    w_mix_bf = w_mix[0].astype(BF16)
    w_o_bf = w_o[0].astype(BF16)
    w_gate_bf = w_gate[0].astype(BF16)
    w_up_bf = w_up[0].astype(BF16)
    w_down_bf = w_down[0].astype(BF16)
    wc = w_conv[0]

    sl = (_alibi_slopes_np().astype(np.float64) * LOG2E).astype(np.float32)
    hi = sl.astype(jnp.bfloat16).astype(np.float32)
    mid = (sl - hi).astype(jnp.bfloat16).astype(np.float32)
    lo = (sl - hi - mid).astype(jnp.bfloat16).astype(np.float32)
    cst_np = np.zeros((N_HEADS, AUG_ROWS, MOBA_BLOCK), np.float32)
    for r, piece in enumerate((hi, mid, lo, hi, mid, lo)):
        cst_np[:, r, :] = piece[:, None]
    cst = jnp.asarray(cst_np, dtype=BF16)

    xs = x_sample[:, 0, :]
    h0 = state_conv[0, :, 0, :]
    h1 = state_conv[0, :, 1, :]
    q_s, k_s, v_s, cn_s, xc_s = _mix_sample(xs, g_mix, w_mix_bf, wc, g_co, h0, h1)
    hs = (N_HEADS, HEAD_DIM)
    lane_bc = lambda t: jnp.broadcast_to(t.reshape((n_seq,) + hs + (1,)), (n_seq,) + hs + (PAGE_SIZE,))
    q_bc, kn_bc, vn_bc = lane_bc(q_s), lane_bc(k_s), lane_bc(v_s)
    cache_kt = jnp.transpose(cache_k, (0, 1, 3, 4, 2))
    cache_vt = jnp.transpose(cache_v, (0, 1, 3, 4, 2))

    k_p, v_p, kb_p, vt_p, qt_p, st_p, cn_p, cst_p = _mix_prompt(x_prompt, g_mix, w_mix_bf, wc, g_co)
    attn_p = _attn_prompt(qt_p, st_p, kb_p.reshape(b, s // MOBA_BLOCK, MOBA_BLOCK, a), vt_p, cst)
    y_p, idx = _ffn(x_prompt.reshape(b * s, d), attn_p.reshape(b * s, a), cn_p.reshape(b * s, c),
                    g_ao, w_o_bf, g_ffn, w_gate_bf, w_up_bf, w_down_bf, g_fin, tm=b * s // n_seq,
                    gate_inputs=(page_table, q_bc, cache_kt))
    y_p = y_p.reshape(b, s, d)

    attn_s = _attn_sample(page_table, idx, q_bc, kn_bc, vn_bc, cache_kt, cache_vt, past_len)
    attn_s = jnp.transpose(attn_s, (0, 2, 1)).reshape(n_seq, a)
    y_s = _ffn(xs, attn_s, cn_s, g_ao, w_o_bf, g_ffn, w_gate_bf, w_up_bf, w_down_bf, g_fin, tm=n_seq)

    return (y_p, y_s[:, None, :],
            k_p.reshape((1, b, s) + hs), v_p.reshape((1, b, s) + hs), cst_p[None],
            k_s.reshape((1, n_seq, 1) + hs), v_s.reshape((1, n_seq, 1) + hs),
            jnp.stack([h1, xc_s], axis=1)[None])
```

```python
import functools

import numpy as np
import jax
import jax.numpy as jnp
from jax import lax
from jax.experimental import pallas as pl
from jax.experimental.pallas import tpu as pltpu

F32 = jnp.float32
BF16 = jnp.bfloat16

HEAD_DIM = 64
N_HEADS = 8
ATTN_WIDTH = N_HEADS * HEAD_DIM
MOBA_BLOCK = 256
MOBA_TOPK = 3
PAGE_SIZE = 128
PAGES_PER_BLOCK = MOBA_BLOCK // PAGE_SIZE
RMS_EPS = 1e-6
ATTN_SCALE = HEAD_DIM ** -0.5
LOG2E = 1.4426950408889634
NEG = -1e30
LANE = 128
SUBLANE = 8
VMEM_LIMIT = 56 * 1024 * 1024


def _alibi_slopes_np():
    return (2.0 ** (-8.0 * np.arange(1, N_HEADS + 1, dtype=np.float64) / N_HEADS)).astype(np.float32)


def _rms(x, g):
    y = x * lax.rsqrt(jnp.mean(x * x, axis=-1, keepdims=True) + RMS_EPS)
    return y * g


def _split_bf16(x):
    hi = x.astype(BF16)
    lo = (x - hi.astype(F32)).astype(BF16)
    return hi, lo


_DN_T = (((1,), (1,)), ((), ()))


MIX_CHUNK = 512


def _mix_project(x_ref, g_ref, w_ref, z_ref, finish=()):
    h = _rms(x_ref[...], g_ref[...]).astype(BF16)
    finish = iter(finish)
    for c0 in range(0, w_ref.shape[1], MIX_CHUNK):
        z_ref[:, c0:c0 + MIX_CHUNK] = jnp.dot(h, w_ref[:, c0:c0 + MIX_CHUNK],
                                              preferred_element_type=F32)
        next(finish, None)
    for _ in finish:
        pass


def _mix_finish(s, z_ref, wc_ref, gco_ref,
                k_ref, v_ref, kb_ref, vt_ref, qt_ref, st_ref, cn_ref, cst_ref, carry_ref, kmean_ref):
    tm = z_ref.shape[0]
    a = ATTN_WIDTH
    c = cn_ref.shape[1]
    first = s == 0

    k = z_ref[:, a:2 * a]
    v = z_ref[:, 2 * a:3 * a]
    k_ref[...] = k
    v_ref[...] = v
    kb_ref[...] = k.astype(BF16)
    vt = jnp.transpose(v)
    vt_ref[...] = vt.astype(BF16)
    ksum = jnp.sum(k, axis=0, keepdims=True) * (1.0 / MOBA_BLOCK)
    yield

    q = z_ref[:, 0:a]
    qt = jnp.transpose(q * (ATTN_SCALE * LOG2E))
    qt_ref[...] = qt.astype(BF16)
    yield

    gb = z_ref[:, 3 * a:3 * a + c]
    gc = z_ref[:, 3 * a + c:3 * a + 2 * c]
    u = z_ref[:, 3 * a + 2 * c:3 * a + 3 * c]
    xc = gc * u
    row = lax.broadcasted_iota(jnp.int32, xc.shape, 0)
    p0 = jnp.where(first, 0.0, carry_ref[0:1, :])
    p1 = jnp.where(first, 0.0, carry_ref[1:2, :])
    xc1 = jnp.where(row == 0, p1, pltpu.roll(xc, 1, axis=0))
    xc2 = jnp.where(row == 0, p0, jnp.where(row == 1, p1, pltpu.roll(xc, 2, axis=0)))
    y = wc_ref[0:1, :] * xc2 + wc_ref[1:2, :] * xc1 + wc_ref[2:3, :] * xc
    cn = _rms(gb * y, gco_ref[...])
    cn_ref[...] = cn.astype(BF16)
    last2 = xc[tm - 2:tm, :]
    carry_ref[...] = last2
    cst_ref[...] = last2
    yield

    nb = kmean_ref.shape[0]
    krow = lax.broadcasted_iota(jnp.int32, kmean_ref.shape, 0)
    km = jnp.where(krow < s, kmean_ref[...], 0.0)
    kmt = jnp.concatenate([km] * N_HEADS, axis=0)
    rr = lax.broadcasted_iota(jnp.int32, kmt.shape, 0) // nb
    cc = lax.broadcasted_iota(jnp.int32, kmt.shape, 1) // HEAD_DIM
    kmt = jnp.where(rr == cc, kmt, 0.0)
    qh, ql = _split_bf16(q)
    kh, kl = _split_bf16(kmt)
    gate = (lax.dot_general(qh, kh, _DN_T, preferred_element_type=F32)
            + lax.dot_general(ql, kh, _DN_T, preferred_element_type=F32)
            + lax.dot_general(qh, kl, _DN_T, preferred_element_type=F32))

    lane = lax.broadcasted_iota(jnp.int32, gate.shape, 1)
    jl = lane % nb
    valid = jl < s
    gm = jnp.where(valid, gate, -jnp.inf)
    cnt = jnp.zeros(gate.shape, F32)
    width = gate.shape[1]
    yield
    for sh in range(1, nb):
        fwd = pltpu.roll(gm, width - sh, axis=1)
        bwd = pltpu.roll(gm, nb - sh, axis=1)
        wrapped = (jl + sh) >= nb
        beats = jnp.where(wrapped, jnp.where(bwd >= gm, 1.0, 0.0), jnp.where(fwd > gm, 1.0, 0.0))
        cnt = cnt + beats
        if sh % 8 == 0:
            yield
    selb = jnp.where(valid, jnp.where(cnt < MOBA_TOPK, 0.0, NEG), NEG)
    selb = jnp.where(jl == s, 0.0, selb)
    st_ref[...] = jnp.transpose(selb).astype(BF16)

    kmean_ref[pl.ds(s, 1), :] = ksum


def _mix_prompt_kernel(x_ref, g_ref, w_ref, wc_ref, gco_ref,
                       k_ref, v_ref, kb_ref, vt_ref, qt_ref, st_ref, cn_ref, cst_ref,
                       carry_ref, kmean_ref, za_ref, zb_ref, *, tiles_per_seq):
    t = pl.program_id(0)
    s_prev = (t + tiles_per_seq - 1) % tiles_per_seq
    outs = (k_ref, v_ref, kb_ref, vt_ref, qt_ref, st_ref, cn_ref, cst_ref, carry_ref, kmean_ref)

    @pl.when(t == 0)
    def _():
        carry_ref[...] = jnp.zeros_like(carry_ref)
        kmean_ref[...] = jnp.zeros_like(kmean_ref)
        _mix_project(x_ref, g_ref, w_ref, za_ref)

    @pl.when(t % 2 == 1)
    def _():
        _mix_project(x_ref, g_ref, w_ref, zb_ref, _mix_finish(s_prev, za_ref, wc_ref, gco_ref, *outs))

    @pl.when((t % 2 == 0) & (t > 0))
    def _():
        _mix_project(x_ref, g_ref, w_ref, za_ref, _mix_finish(s_prev, zb_ref, wc_ref, gco_ref, *outs))


def _mix_prompt(x, g, w_bf, wc, gco):
    b, s, d = x.shape
    tm = MOBA_BLOCK
    ns = s // tm
    nt = b * ns
    a = ATTN_WIDTH
    c = wc.shape[1]
    assert s % tm == 0 and ns * N_HEADS == LANE and w_bf.shape[1] == 3 * a + 3 * c and nt % 2 == 0
    tin = lambda t: jnp.minimum(t, nt - 1)
    tout = lambda t: jnp.maximum(t - 1, 0)
    row_in = lambda n: pl.BlockSpec((None, tm, n), lambda t: (tin(t) // ns, tin(t) % ns, 0))
    row_spec = lambda n: pl.BlockSpec((None, tm, n), lambda t: (tout(t) // ns, tout(t) % ns, 0))
    col_spec = lambda n: pl.BlockSpec((None, n, tm), lambda t: (tout(t) // ns, 0, tout(t) % ns))
    full = lambda shp: pl.BlockSpec(shp, lambda t: (0,) * len(shp))
    return pl.pallas_call(
        functools.partial(_mix_prompt_kernel, tiles_per_seq=ns),
        grid=(nt + 1,),
        in_specs=[row_in(d), full((1, d)), full(w_bf.shape), full(wc.shape), full((1, c))],
        out_specs=[row_spec(a), row_spec(a), row_spec(a),
                   pl.BlockSpec((None, None, a, tm), lambda t: (tout(t) // ns, tout(t) % ns, 0, 0)),
                   col_spec(a), col_spec(LANE), row_spec(c),
                   pl.BlockSpec((None, 2, c), lambda t: (tout(t) // ns, 0, 0))],
        out_shape=[jax.ShapeDtypeStruct((b, s, a), F32),
                   jax.ShapeDtypeStruct((b, s, a), F32),
                   jax.ShapeDtypeStruct((b, s, a), BF16),
                   jax.ShapeDtypeStruct((b, ns, a, tm), BF16),
                   jax.ShapeDtypeStruct((b, a, s), BF16),
                   jax.ShapeDtypeStruct((b, LANE, s), BF16),
                   jax.ShapeDtypeStruct((b, s, c), BF16),
                   jax.ShapeDtypeStruct((b, 2, c), F32)],
        scratch_shapes=[pltpu.VMEM((2, c), F32), pltpu.VMEM((ns, a), F32),
                        pltpu.VMEM((tm, w_bf.shape[1]), F32), pltpu.VMEM((tm, w_bf.shape[1]), F32)],
        compiler_params=pltpu.CompilerParams(dimension_semantics=("arbitrary",),
                                             vmem_limit_bytes=VMEM_LIMIT),
        name="mix_prompt",
    )(x, g, w_bf, wc, gco)


AUG_ROWS = 16


QK_LEAD = 2


def _attn_prompt_kernel(cst_ref, qt_ref, st_ref, kb_ref, vt_ref, o_ref, w_ref, sc_ref, *state_refs):
    i = pl.program_id(1)
    blk = MOBA_BLOCK
    hd = HEAD_DIM

    zq = jnp.zeros((hd, blk), BF16)
    zpad = jnp.zeros((LANE - 2 * AUG_ROWS, blk), BF16)
    for h in range(N_HEADS):
        qh = qt_ref[h * hd:(h + 1) * hd, :]
        qpart = [qh, zq] if h % 2 == 0 else [zq, qh]
        w_ref[h] = jnp.concatenate(
            qpart + [st_ref[h * AUG_ROWS:(h + 1) * AUG_ROWS, :], cst_ref[h], zpad], axis=0)

    elane = lax.broadcasted_iota(jnp.int32, (blk, LANE), 1)
    erow = lax.broadcasted_iota(jnp.int32, (blk, LANE), 0).astype(F32)
    ebase = jnp.where((elane >= AUG_ROWS) & (elane < AUG_ROWS + 3), erow, 0.0)
    rk = lax.broadcasted_iota(jnp.int32, (blk, blk), 0)
    rq = lax.broadcasted_iota(jnp.int32, (blk, blk), 1)
    ones_rows = jnp.ones((2 * SUBLANE, blk), BF16)

    def key_aug(j):
        off = ((j - i) * blk).astype(F32)
        return jnp.where(elane == j, 1.0,
                         jnp.where((elane >= AUG_ROWS + 3) & (elane < AUG_ROWS + 6), off, ebase)).astype(BF16)

    def scores(kbj, e, h):
        rhs = jnp.concatenate([kbj[:, (h // 2) * LANE:(h // 2 + 1) * LANE], e], axis=1)
        return jnp.dot(rhs, w_ref[h], preferred_element_type=F32)

    def block(j, nxt, diagonal, rd, wr):
        vtj = vt_ref[j]
        if nxt is not None:
            kbn = kb_ref[nxt]
            en = key_aug(nxt)

        def score_ahead(h):
            if nxt is not None and h < N_HEADS:
                sc_ref[wr, h] = scores(kbn, en, h)

        for h in range(QK_LEAD):
            score_ahead(h)
        for h in range(N_HEADS):
            st = sc_ref[rd, h]
            score_ahead(h + QK_LEAD)
            if diagonal:
                st = st + jnp.where(rk <= rq, 0.0, NEG)
            acc_ref, m_ref, l_ref = state_refs[h], state_refs[N_HEADS + h], state_refs[2 * N_HEADS + h]
            m_old = m_ref[...]
            m_new = jnp.maximum(m_old, jnp.max(st, axis=0, keepdims=True))
            alpha = jnp.exp2(m_old - m_new)
            p = jnp.exp2(st - m_new)
            m_ref[...] = m_new
            pv = jnp.dot(jnp.concatenate([vtj[h * hd:(h + 1) * hd, :], ones_rows], axis=0),
                         p.astype(BF16), preferred_element_type=F32)
            l_ref[...] = alpha * l_ref[...] + pv[hd:hd + 1, :]
            acc_ref[...] = alpha * acc_ref[...] + pv[0:hd, :]

    kb0 = kb_ref[0]
    e0 = key_aug(0)
    for h in range(N_HEADS):
        sc_ref[0, h] = scores(kb0, e0, h)
        state_refs[h][...] = jnp.zeros((hd, blk), F32)
        state_refs[N_HEADS + h][...] = jnp.full((1, blk), -jnp.inf, F32)
        state_refs[2 * N_HEADS + h][...] = jnp.zeros((1, blk), F32)

    def pair(t, carry):
        block(2 * t, 2 * t + 1, False, 0, 1)
        block(2 * t + 1, 2 * t + 2, False, 1, 0)
        return carry

    lax.fori_loop(0, i // 2, pair, 0)

    def finish(rd):
        block(i, None, True, rd, None)
        o_t = jnp.concatenate(
            [state_refs[h][...] / state_refs[2 * N_HEADS + h][...] for h in range(N_HEADS)], axis=0)
        o_ref[...] = jnp.transpose(o_t)

    @pl.when(i % 2 == 0)
    def _():
        finish(0)

    @pl.when(i % 2 == 1)
    def _():
        block(i - 1, i, False, 0, 1)
        finish(1)


def _attn_prompt(qt, st, kb, vt, cst):
    b, nb, blk, a = kb.shape
    assert blk == MOBA_BLOCK and nb <= AUG_ROWS and a == ATTN_WIDTH and st.shape[1] == N_HEADS * AUG_ROWS
    return pl.pallas_call(
        _attn_prompt_kernel,
        grid=(b, nb),
        in_specs=[pl.BlockSpec(cst.shape, lambda bi, i: (0, 0, 0)),
                  pl.BlockSpec((None, a, blk), lambda bi, i: (bi, 0, i)),
                  pl.BlockSpec((None, N_HEADS * AUG_ROWS, blk), lambda bi, i: (bi, 0, i)),
                  pl.BlockSpec((None, nb, blk, a), lambda bi, i: (bi, 0, 0, 0)),
                  pl.BlockSpec((None, nb, a, blk), lambda bi, i: (bi, 0, 0, 0))],
        out_specs=pl.BlockSpec((None, blk, a), lambda bi, i: (bi, i, 0)),
        out_shape=jax.ShapeDtypeStruct((b, nb * blk, a), F32),
        scratch_shapes=([pltpu.VMEM((N_HEADS, 2 * LANE, blk), BF16),
                         pltpu.VMEM((2, N_HEADS, blk, blk), F32)]
                        + [pltpu.VMEM((HEAD_DIM, blk), F32)] * N_HEADS
                        + [pltpu.VMEM((1, blk), F32)] * (2 * N_HEADS)),
        compiler_params=pltpu.CompilerParams(
            dimension_semantics=("arbitrary", "arbitrary"), vmem_limit_bytes=VMEM_LIMIT),
        name="attn_prompt",
    )(cst, qt, st, kb, vt)


class _NoSide:
    def region(self):
        pass

    def close(self):
        pass


MXU_TILE = 2 * LANE


def _ffn_body(x_ref, a_ref, cn_ref, gao_ref, wo_ref, gffn_ref, wg_ref, wu_ref, wd_ref, gfin_ref,
              y_ref, side):
    ff = wg_ref.shape[1]
    split = (ff // 2 + MXU_TILE - 1) // MXU_TILE * MXU_TILE
    side.region()
    an = _rms(a_ref[...], gao_ref[...]).astype(BF16)
    mixed = jnp.concatenate([an, cn_ref[...]], axis=1)
    x1 = x_ref[...] + jnp.dot(mixed, wo_ref[...], preferred_element_type=F32)
    h2 = _rms(x1, gffn_ref[...]).astype(BF16)
    x2 = x1
    for lo, hi in ((0, split), (split, ff)):
        side.region()
        gate = jnp.dot(h2, wg_ref[:, lo:hi], preferred_element_type=F32)
        side.region()
        up = jnp.dot(h2, wu_ref[:, lo:hi], preferred_element_type=F32)
        act = (gate * (1.0 / (1.0 + jnp.exp(-gate))) * up).astype(BF16)
        side.region()
        x2 = x2 + jnp.dot(act, wd_ref[lo:hi, :], preferred_element_type=F32)
    side.region()
    y_ref[...] = _rms(x2, gfin_ref[...])
    side.close()


N_FFN_PARTS = 8
N_RING = 32


class _GateStream:
    def __init__(self, b, pt_ref, q_ref, ck_ref, idx_ref, buf_ref, sem_ref, part_ref):
        self.b, self.pt_ref, self.q_ref, self.ck_ref = b, pt_ref, q_ref, ck_ref
        self.idx_ref, self.buf_ref, self.sem_ref, self.part_ref = idx_ref, buf_ref, sem_ref, part_ref
        self.n_seq, self.n_pages = pt_ref.shape
        self.per_slice = self.n_pages // N_FFN_PARTS
        self.k = -1

        @pl.when(b == 0)
        def _():
            for page in range(N_RING):
                self._copy(0, page).start()

    def _copy(self, seq, page):
        return pltpu.make_async_copy(self.ck_ref.at[0, self.pt_ref[seq, page]],
                                     self.buf_ref.at[page % N_RING], self.sem_ref.at[page % N_RING])

    def _pages(self):
        return range(self.k * self.per_slice, (self.k + 1) * self.per_slice)

    def region(self):
        self.k += 1
        pages = self._pages()
        for page in pages:
            self._copy(self.b, page).wait()
        self._reduce(range(pages.start // PAGES_PER_BLOCK, pages.stop // PAGES_PER_BLOCK))
        self._refill(pages)

    def _reduce(self, blocks):
        for h in range(N_HEADS):
            qh = self.q_ref[h]
            for n in blocks:
                ksum = None
                for half in range(PAGES_PER_BLOCK):
                    page = self.buf_ref[(n * PAGES_PER_BLOCK + half) % N_RING, h]
                    ksum = page if ksum is None else ksum + page
                prod = ksum * qh
                self.part_ref[n, h] = jnp.sum(
                    prod.reshape(HEAD_DIM // SUBLANE, SUBLANE, PAGE_SIZE), axis=0)

    def _refill(self, pages):
        if pages.stop + N_RING <= self.n_pages:
            for page in pages:
                self._copy(self.b, page + N_RING).start()
        else:

            @pl.when(self.b + 1 < self.n_seq)
            def _():
                for page in pages:
                    self._copy(self.b + 1, page + N_RING - self.n_pages).start()

    def close(self):
        assert self.k == N_FFN_PARTS - 1
        nblk = self.n_pages // PAGES_PER_BLOCK
        g = jnp.sum(self.part_ref[...], axis=(2, 3))
        nidx = lax.broadcasted_iota(jnp.int32, g.shape, 0)
        picks = []
        for _ in range(MOBA_TOPK):
            mx = jnp.max(g, axis=0, keepdims=True)
            pick = jnp.min(jnp.where(g == mx, nidx, nblk), axis=0, keepdims=True)
            picks.append(pick)
            g = jnp.where(nidx == pick, -jnp.inf, g)
        self.idx_ref[...] = jnp.concatenate(picks, axis=0)


def _ffn_kernel(*refs, with_gate):
    if with_gate:
        pt_ref, q_ref, ck_ref = refs[0], refs[11], refs[12]
        ffn_refs = refs[1:11] + refs[13:14]
        side = _GateStream(pl.program_id(0), pt_ref, q_ref, ck_ref, *refs[14:18])
    else:
        ffn_refs = refs
        side = _NoSide()
    _ffn_body(*ffn_refs, side)


def _ffn(x, attn, cn, gao, wo, gffn, wg, wu, wd, gfin, tm, gate_inputs=None):
    n, d = x.shape
    a = attn.shape[1]
    c = cn.shape[1]
    assert n % tm == 0
    row = lambda w: pl.BlockSpec((tm, w), lambda i, *_: (i, 0))
    res = lambda arr: pl.BlockSpec(arr.shape, lambda i, *_: (0, 0), pipeline_mode=pl.Buffered(1))
    in_specs = [row(d), row(a), row(c), res(gao), res(wo), res(gffn), res(wg), res(wu), res(wd),
                res(gfin)]
    out_specs = [row(d)]
    out_shape = [jax.ShapeDtypeStruct((n, d), F32)]
    scratch = []
    args = (x, attn, cn, gao, wo, gffn, wg, wu, wd, gfin)
    if gate_inputs is not None:
        page_table, q_bc, cache_kt = gate_inputs
        n_seq, n_pages = page_table.shape
        nblk = n_pages // PAGES_PER_BLOCK
        per_slice = n_pages // N_FFN_PARTS
        assert n // tm == n_seq and n_pages % N_FFN_PARTS == 0 and per_slice % PAGES_PER_BLOCK == 0
        assert N_RING % per_slice == 0 and n_pages % N_RING == 0 and nblk >= MOBA_TOPK
        assert cache_kt.shape[2:] == (N_HEADS, HEAD_DIM, PAGE_SIZE) and HEAD_DIM % SUBLANE == 0
        in_specs += [pl.BlockSpec((None, N_HEADS, HEAD_DIM, PAGE_SIZE), lambda i, *_: (i, 0, 0, 0)),
                     pl.BlockSpec(memory_space=pl.ANY)]
        out_specs += [pl.BlockSpec((None, MOBA_TOPK, N_HEADS), lambda i, *_: (i, 0, 0))]
        out_shape += [jax.ShapeDtypeStruct((n_seq, MOBA_TOPK, N_HEADS), jnp.int32)]
        scratch = [pltpu.VMEM((N_RING, N_HEADS, HEAD_DIM, PAGE_SIZE), F32),
                   pltpu.SemaphoreType.DMA((N_RING,)),
                   pltpu.VMEM((nblk, N_HEADS, SUBLANE, PAGE_SIZE), F32)]
        args = (page_table,) + args + (q_bc, cache_kt)
    grid_spec = pltpu.PrefetchScalarGridSpec(
        num_scalar_prefetch=0 if gate_inputs is None else 1,
        grid=(n // tm,), in_specs=in_specs, out_specs=out_specs, scratch_shapes=scratch)
    out = pl.pallas_call(
        functools.partial(_ffn_kernel, with_gate=gate_inputs is not None),
        grid_spec=grid_spec,
        out_shape=out_shape,
        compiler_params=pltpu.CompilerParams(dimension_semantics=("arbitrary",),
                                             vmem_limit_bytes=VMEM_LIMIT),
        name="ffn_%d" % n,
    )(*args)
    return out[0] if gate_inputs is None else out


def _mix_sample_kernel(x_ref, g_ref, w_ref, wc_ref, gco_ref, h0_ref, h1_ref,
                       q_ref, k_ref, v_ref, cn_ref, xc_ref):
    a = ATTN_WIDTH
    c = cn_ref.shape[1]
    h = _rms(x_ref[...], g_ref[...]).astype(BF16)
    z = jnp.dot(h, w_ref[...], preferred_element_type=F32)
    q_ref[...] = z[:, 0:a]
    k_ref[...] = z[:, a:2 * a]
    v_ref[...] = z[:, 2 * a:3 * a]
    gb = z[:, 3 * a:3 * a + c]
    xc = z[:, 3 * a + c:3 * a + 2 * c] * z[:, 3 * a + 2 * c:3 * a + 3 * c]
    y = wc_ref[0:1, :] * h0_ref[...] + wc_ref[1:2, :] * h1_ref[...] + wc_ref[2:3, :] * xc
    cn_ref[...] = _rms(gb * y, gco_ref[...]).astype(BF16)
    xc_ref[...] = xc


def _mix_sample(x, g, w_bf, wc, gco, h0, h1):
    n = x.shape[0]
    a = ATTN_WIDTH
    c = wc.shape[1]
    return pl.pallas_call(
        _mix_sample_kernel,
        out_shape=[jax.ShapeDtypeStruct((n, a), F32)] * 3
        + [jax.ShapeDtypeStruct((n, c), BF16), jax.ShapeDtypeStruct((n, c), F32)],
        compiler_params=pltpu.CompilerParams(vmem_limit_bytes=VMEM_LIMIT),
        name="mix_sample",
    )(x, g, w_bf, wc, gco, h0, h1)


def _attn_sample_kernel(pt_ref, idx_ref, q_ref, kn_ref, vn_ref, ck_ref, cv_ref, o_ref,
                        kbuf_ref, vbuf_ref, sem_ref, *, past_len, slopes):
    b = pl.program_id(0)
    n_seq = pl.num_programs(0)
    n_slab = MOBA_TOPK * PAGES_PER_BLOCK
    slot = b % 2

    def copies(seq, sl):
        cps = []
        for h in range(N_HEADS):
            for t in range(MOBA_TOPK):
                blk = idx_ref[seq, t, h]
                for half in range(PAGES_PER_BLOCK):
                    pg = pt_ref[seq, blk * PAGES_PER_BLOCK + half]
                    u = t * PAGES_PER_BLOCK + half
                    cps.append(pltpu.make_async_copy(ck_ref.at[0, pg, h], kbuf_ref.at[sl, h, u],
                                                     sem_ref.at[0, sl]))
                    cps.append(pltpu.make_async_copy(cv_ref.at[0, pg, h], vbuf_ref.at[sl, h, u],
                                                     sem_ref.at[1, sl]))
        return cps

    @pl.when(b == 0)
    def _():
        for cp in copies(0, 0):
            cp.start()

    @pl.when(b + 1 < n_seq)
    def _():
        for cp in copies(b + 1, 1 - slot):
            cp.start()

    for cp in copies(b, slot):
        cp.wait()

    pos = lax.broadcasted_iota(jnp.int32, (1, PAGE_SIZE), 1)
    for h in range(N_HEADS):
        qh = q_ref[h] * ATTN_SCALE
        own = jnp.sum(qh * kn_ref[h], axis=0, keepdims=True)
        rows = []
        for t in range(MOBA_TOPK):
            blk = idx_ref[b, t, h]
            for half in range(PAGES_PER_BLOCK):
                u = t * PAGES_PER_BLOCK + half
                kpos = blk * MOBA_BLOCK + half * PAGE_SIZE + pos
                dist = (past_len - kpos).astype(F32)
                rows.append(jnp.sum(kbuf_ref[slot, h, u] * qh, axis=0, keepdims=True)
                            - float(slopes[h]) * dist)
        mrow = functools.reduce(jnp.maximum, rows)
        m = jnp.maximum(jnp.max(mrow, axis=1, keepdims=True), own)
        p_own = jnp.exp(own - m)
        den = p_own
        acc = jnp.zeros((HEAD_DIM, PAGE_SIZE), F32)
        for u in range(n_slab):
            p = jnp.exp(rows[u] - m)
            den = den + jnp.sum(p, axis=1, keepdims=True)
            acc = acc + vbuf_ref[slot, h, u] * p
        num = jnp.sum(acc, axis=1, keepdims=True) + (p_own * vn_ref[h])[:, 0:1]
        o_ref[:, h:h + 1] = num / den[:, 0:1]


def _attn_sample(page_table, idx, q_bc, kn_bc, vn_bc, cache_kt, cache_vt, past_len):
    n_seq = q_bc.shape[0]
    blk4 = pl.BlockSpec((None, N_HEADS, HEAD_DIM, PAGE_SIZE), lambda i, pt, ix: (i, 0, 0, 0))
    n_slab = MOBA_TOPK * PAGES_PER_BLOCK
    grid_spec = pltpu.PrefetchScalarGridSpec(
        num_scalar_prefetch=2,
        grid=(n_seq,),
        in_specs=[blk4, blk4, blk4, pl.BlockSpec(memory_space=pl.ANY), pl.BlockSpec(memory_space=pl.ANY)],
        out_specs=pl.BlockSpec((None, HEAD_DIM, N_HEADS), lambda i, pt, ix: (i, 0, 0)),
        scratch_shapes=[pltpu.VMEM((2, N_HEADS, n_slab, HEAD_DIM, PAGE_SIZE), F32),
                        pltpu.VMEM((2, N_HEADS, n_slab, HEAD_DIM, PAGE_SIZE), F32),
                        pltpu.SemaphoreType.DMA((2, 2))])
    kern = functools.partial(_attn_sample_kernel, past_len=past_len, slopes=_alibi_slopes_np())
    return pl.pallas_call(
        kern,
        grid_spec=grid_spec,
        out_shape=jax.ShapeDtypeStruct((n_seq, HEAD_DIM, N_HEADS), F32),
        compiler_params=pltpu.CompilerParams(dimension_semantics=("arbitrary",),
                                             vmem_limit_bytes=VMEM_LIMIT),
        name="attn_sample",
    )(page_table, idx, q_bc, kn_bc, vn_bc, cache_kt, cache_vt)


def kernel(x_prompt, x_sample, cache_k, cache_v, state_conv, page_table, norm_mix, w_mix, w_conv,
           norm_attn_out, norm_conv_out, w_o, norm_ffn, w_gate, w_up, w_down, norm_final):
    depth = w_mix.shape[0]
    assert depth == 1 and x_sample.shape[1] == 1
    b, s, d = x_prompt.shape
    n_seq = x_sample.shape[0]
    n_pages = page_table.shape[1]
    past_len = n_pages * PAGE_SIZE
    a = ATTN_WIDTH
    c = w_conv.shape[2]

    g_mix = norm_mix[0][None, :]
    g_ao = norm_attn_out[0][None, :]
    g_co = norm_conv_out[0][None, :]
    g_ffn = norm_ffn[0][None, :]
    g_fin = norm_final[None, :]
    w_mix_bf = w_mix[0].astype(BF16)
    w_o_bf = w_o[0].astype(BF16)
    w_gate_bf = w_gate[0].astype(BF16)
    w_up_bf = w_up[0].astype(BF16)
    w_down_bf = w_down[0].astype(BF16)
    wc = w_conv[0]

    sl = (_alibi_slopes_np().astype(np.float64) * LOG2E).astype(np.float32)
    hi = sl.astype(jnp.bfloat16).astype(np.float32)
    mid = (sl - hi).astype(jnp.bfloat16).astype(np.float32)
    lo = (sl - hi - mid).astype(jnp.bfloat16).astype(np.float32)
    cst_np = np.zeros((N_HEADS, AUG_ROWS, MOBA_BLOCK), np.float32)
    for r, piece in enumerate((hi, mid, lo, hi, mid, lo)):
        cst_np[:, r, :] = piece[:, None]
    cst = jnp.asarray(cst_np, dtype=BF16)

    xs = x_sample[:, 0, :]
    h0 = state_conv[0, :, 0, :]
    h1 = state_conv[0, :, 1, :]
    q_s, k_s, v_s, cn_s, xc_s = _mix_sample(xs, g_mix, w_mix_bf, wc, g_co, h0, h1)
    hs = (N_HEADS, HEAD_DIM)
    lane_bc = lambda t: jnp.broadcast_to(t.reshape((n_seq,) + hs + (1,)), (n_seq,) + hs + (PAGE_SIZE,))
    q_bc, kn_bc, vn_bc = lane_bc(q_s), lane_bc(k_s), lane_bc(v_s)
    cache_kt = jnp.transpose(cache_k, (0, 1, 3, 4, 2))
    cache_vt = jnp.transpose(cache_v, (0, 1, 3, 4, 2))

    k_p, v_p, kb_p, vt_p, qt_p, st_p, cn_p, cst_p = _mix_prompt(x_prompt, g_mix, w_mix_bf, wc, g_co)
    attn_p = _attn_prompt(qt_p, st_p, kb_p.reshape(b, s // MOBA_BLOCK, MOBA_BLOCK, a), vt_p, cst)
    y_p, idx = _ffn(x_prompt.reshape(b * s, d), attn_p.reshape(b * s, a), cn_p.reshape(b * s, c),
                    g_ao, w_o_bf, g_ffn, w_gate_bf, w_up_bf, w_down_bf, g_fin, tm=b * s // n_seq,
                    gate_inputs=(page_table, q_bc, cache_kt))
    y_p = y_p.reshape(b, s, d)

    attn_s = _attn_sample(page_table, idx, q_bc, kn_bc, vn_bc, cache_kt, cache_vt, past_len)
    attn_s = jnp.transpose(attn_s, (0, 2, 1)).reshape(n_seq, a)
    y_s = _ffn(xs, attn_s, cn_s, g_ao, w_o_bf, g_ffn, w_gate_bf, w_up_bf, w_down_bf, g_fin, tm=n_seq)

    return (y_p, y_s[:, None, :],
            k_p.reshape((1, b, s) + hs), v_p.reshape((1, b, s) + hs), cst_p[None],
            k_s.reshape((1, n_seq, 1) + hs), v_s.reshape((1, n_seq, 1) + hs),
            jnp.stack([h1, xc_s], axis=1)[None])
```

```python
import functools

import numpy as np
import jax
import jax.numpy as jnp
from jax import lax
from jax.experimental import pallas as pl
from jax.experimental.pallas import tpu as pltpu

F32 = jnp.float32
BF16 = jnp.bfloat16

HEAD_DIM = 64
N_HEADS = 8
ATTN_WIDTH = N_HEADS * HEAD_DIM
MOBA_BLOCK = 256
MOBA_TOPK = 3
PAGE_SIZE = 128
PAGES_PER_BLOCK = MOBA_BLOCK // PAGE_SIZE
RMS_EPS = 1e-6
ATTN_SCALE = HEAD_DIM ** -0.5
LOG2E = 1.4426950408889634
NEG = -1e30
LANE = 128
SUBLANE = 8
VMEM_LIMIT = 56 * 1024 * 1024


def _alibi_slopes_np():
    return (2.0 ** (-8.0 * np.arange(1, N_HEADS + 1, dtype=np.float64) / N_HEADS)).astype(np.float32)


def _rms(x, g):
    y = x * lax.rsqrt(jnp.mean(x * x, axis=-1, keepdims=True) + RMS_EPS)
    return y * g


def _split_bf16(x):
    hi = x.astype(BF16)
    lo = (x - hi.astype(F32)).astype(BF16)
    return hi, lo


_DN_T = (((1,), (1,)), ((), ()))


MIX_CHUNK = 512


def _mix_project(x_ref, g_ref, w_ref, z_ref, finish=()):
    h = _rms(x_ref[...], g_ref[...]).astype(BF16)
    finish = iter(finish)
    for c0 in range(0, w_ref.shape[1], MIX_CHUNK):
        z_ref[:, c0:c0 + MIX_CHUNK] = jnp.dot(h, w_ref[:, c0:c0 + MIX_CHUNK],
                                              preferred_element_type=F32)
        next(finish, None)
    for _ in finish:
        pass


def _mix_finish(s, z_ref, wc_ref, gco_ref,
                k_ref, v_ref, kb_ref, vt_ref, qt_ref, st_ref, cn_ref, cst_ref, carry_ref, kmean_ref):
    tm = z_ref.shape[0]
    a = ATTN_WIDTH
    c = cn_ref.shape[1]
    first = s == 0

    k = z_ref[:, a:2 * a]
    v = z_ref[:, 2 * a:3 * a]
    k_ref[...] = k
    v_ref[...] = v
    kb_ref[...] = k.astype(BF16)
    vt = jnp.transpose(v)
    vt_ref[...] = vt.astype(BF16)
    ksum = jnp.sum(k, axis=0, keepdims=True) * (1.0 / MOBA_BLOCK)
    yield

    qt = jnp.transpose(z_ref[:, 0:a])
    qt_ref[...] = (qt * (ATTN_SCALE * LOG2E)).astype(BF16)
    yield

    gb = z_ref[:, 3 * a:3 * a + c]
    gc = z_ref[:, 3 * a + c:3 * a + 2 * c]
    u = z_ref[:, 3 * a + 2 * c:3 * a + 3 * c]
    xc = gc * u
    row = lax.broadcasted_iota(jnp.int32, xc.shape, 0)
    p0 = jnp.where(first, 0.0, carry_ref[0:1, :])
    p1 = jnp.where(first, 0.0, carry_ref[1:2, :])
    xc1 = jnp.where(row == 0, p1, pltpu.roll(xc, 1, axis=0))
    xc2 = jnp.where(row == 0, p0, jnp.where(row == 1, p1, pltpu.roll(xc, 2, axis=0)))
    y = wc_ref[0:1, :] * xc2 + wc_ref[1:2, :] * xc1 + wc_ref[2:3, :] * xc
    cn = _rms(gb * y, gco_ref[...])
    cn_ref[...] = cn.astype(BF16)
    last2 = xc[tm - 2:tm, :]
    carry_ref[...] = last2
    cst_ref[...] = last2
    yield

    nb = kmean_ref.shape[0]
    hmask = (lax.broadcasted_iota(jnp.int32, (N_HEADS, a), 1) // HEAD_DIM
             == lax.broadcasted_iota(jnp.int32, (N_HEADS, a), 0))
    kmt = jnp.where(hmask[None], kmean_ref[...][:, None, :], 0.0).reshape(nb * N_HEADS, a)
    qh, ql = _split_bf16(qt)
    kh, kl = _split_bf16(kmt)
    gate = (jnp.dot(kh, qh, preferred_element_type=F32) + jnp.dot(kh, ql, preferred_element_type=F32)
            + jnp.dot(kl, qh, preferred_element_type=F32))
    yield

    gm = [jnp.where(j < s, gate[j * N_HEADS:(j + 1) * N_HEADS, :], -jnp.inf) for j in range(nb)]
    cnt = [jnp.zeros((N_HEADS, tm), F32) for _ in range(nb)]
    for j in range(nb):
        for m in range(j):
            m_beats_j = gm[m] >= gm[j]
            cnt[j] = cnt[j] + jnp.where(m_beats_j, 1.0, 0.0)
            cnt[m] = cnt[m] + jnp.where(m_beats_j, 0.0, 1.0)
        if j == nb // 2:
            yield
    sel = [jnp.where(j < s, jnp.where(cnt[j] < MOBA_TOPK, 0.0, NEG), jnp.where(j == s, 0.0, NEG))
           for j in range(nb)]
    st_ref[...] = jnp.concatenate([sel[j][h:h + 1, :] for h in range(N_HEADS) for j in range(nb)],
                                  axis=0).astype(BF16)

    kmean_ref[pl.ds(s, 1), :] = ksum


def _mix_prompt_kernel(x_ref, g_ref, w_ref, wc_ref, gco_ref,
                       k_ref, v_ref, kb_ref, vt_ref, qt_ref, st_ref, cn_ref, cst_ref,
                       carry_ref, kmean_ref, za_ref, zb_ref, *, tiles_per_seq):
    t = pl.program_id(0)
    s_prev = (t + tiles_per_seq - 1) % tiles_per_seq
    outs = (k_ref, v_ref, kb_ref, vt_ref, qt_ref, st_ref, cn_ref, cst_ref, carry_ref, kmean_ref)

    @pl.when(t == 0)
    def _():
        carry_ref[...] = jnp.zeros_like(carry_ref)
        kmean_ref[...] = jnp.zeros_like(kmean_ref)
        _mix_project(x_ref, g_ref, w_ref, za_ref)

    @pl.when(t % 2 == 1)
    def _():
        _mix_project(x_ref, g_ref, w_ref, zb_ref, _mix_finish(s_prev, za_ref, wc_ref, gco_ref, *outs))

    @pl.when((t % 2 == 0) & (t > 0))
    def _():
        _mix_project(x_ref, g_ref, w_ref, za_ref, _mix_finish(s_prev, zb_ref, wc_ref, gco_ref, *outs))


def _mix_prompt(x, g, w_bf, wc, gco):
    b, s, d = x.shape
    tm = MOBA_BLOCK
    ns = s // tm
    nt = b * ns
    a = ATTN_WIDTH
    c = wc.shape[1]
    assert s % tm == 0 and ns * N_HEADS == LANE and w_bf.shape[1] == 3 * a + 3 * c and nt % 2 == 0
    tin = lambda t: jnp.minimum(t, nt - 1)
    tout = lambda t: jnp.maximum(t - 1, 0)
    row_in = lambda n: pl.BlockSpec((None, tm, n), lambda t: (tin(t) // ns, tin(t) % ns, 0))
    row_spec = lambda n: pl.BlockSpec((None, tm, n), lambda t: (tout(t) // ns, tout(t) % ns, 0))
    col_spec = lambda n: pl.BlockSpec((None, n, tm), lambda t: (tout(t) // ns, 0, tout(t) % ns))
    full = lambda shp: pl.BlockSpec(shp, lambda t: (0,) * len(shp))
    return pl.pallas_call(
        functools.partial(_mix_prompt_kernel, tiles_per_seq=ns),
        grid=(nt + 1,),
        in_specs=[row_in(d), full((1, d)), full(w_bf.shape), full(wc.shape), full((1, c))],
        out_specs=[row_spec(a), row_spec(a), row_spec(a),
                   pl.BlockSpec((None, None, a, tm), lambda t: (tout(t) // ns, tout(t) % ns, 0, 0)),
                   col_spec(a), col_spec(LANE), row_spec(c),
                   pl.BlockSpec((None, 2, c), lambda t: (tout(t) // ns, 0, 0))],
        out_shape=[jax.ShapeDtypeStruct((b, s, a), F32),
                   jax.ShapeDtypeStruct((b, s, a), F32),
                   jax.ShapeDtypeStruct((b, s, a), BF16),
                   jax.ShapeDtypeStruct((b, ns, a, tm), BF16),
                   jax.ShapeDtypeStruct((b, a, s), BF16),
                   jax.ShapeDtypeStruct((b, LANE, s), BF16),
                   jax.ShapeDtypeStruct((b, s, c), BF16),
                   jax.ShapeDtypeStruct((b, 2, c), F32)],
        scratch_shapes=[pltpu.VMEM((2, c), F32), pltpu.VMEM((ns, a), F32),
                        pltpu.VMEM((tm, w_bf.shape[1]), F32), pltpu.VMEM((tm, w_bf.shape[1]), F32)],
        compiler_params=pltpu.CompilerParams(dimension_semantics=("arbitrary",),
                                             vmem_limit_bytes=VMEM_LIMIT),
        name="mix_prompt",
    )(x, g, w_bf, wc, gco)


AUG_ROWS = 16


QK_LEAD = 2


def _attn_prompt_kernel(cst_ref, qt_ref, st_ref, kb_ref, vt_ref, o_ref, w_ref, sc_ref, *state_refs):
    i = pl.program_id(1)
    blk = MOBA_BLOCK
    hd = HEAD_DIM

    zq = jnp.zeros((hd, blk), BF16)
    zpad = jnp.zeros((LANE - 2 * AUG_ROWS, blk), BF16)
    for h in range(N_HEADS):
        qh = qt_ref[h * hd:(h + 1) * hd, :]
        qpart = [qh, zq] if h % 2 == 0 else [zq, qh]
        w_ref[h] = jnp.concatenate(
            qpart + [st_ref[h * AUG_ROWS:(h + 1) * AUG_ROWS, :], cst_ref[h], zpad], axis=0)

    elane = lax.broadcasted_iota(jnp.int32, (blk, LANE), 1)
    erow = lax.broadcasted_iota(jnp.int32, (blk, LANE), 0).astype(F32)
    ebase = jnp.where((elane >= AUG_ROWS) & (elane < AUG_ROWS + 3), erow, 0.0)
    rk = lax.broadcasted_iota(jnp.int32, (blk, blk), 0)
    rq = lax.broadcasted_iota(jnp.int32, (blk, blk), 1)
    ones_rows = jnp.ones((2 * SUBLANE, blk), BF16)

    def key_aug(j):
        off = ((j - i) * blk).astype(F32)
        return jnp.where(elane == j, 1.0,
                         jnp.where((elane >= AUG_ROWS + 3) & (elane < AUG_ROWS + 6), off, ebase)).astype(BF16)

    def scores(kbj, e, h):
        rhs = jnp.concatenate([kbj[:, (h // 2) * LANE:(h // 2 + 1) * LANE], e], axis=1)
        return jnp.dot(rhs, w_ref[h], preferred_element_type=F32)

    def block(j, nxt, diagonal, rd, wr):
        vtj = vt_ref[j]
        if nxt is not None:
            kbn = kb_ref[nxt]
            en = key_aug(nxt)

        def score_ahead(h):
            if nxt is not None and h < N_HEADS:
                sc_ref[wr, h] = scores(kbn, en, h)

        for h in range(QK_LEAD):
            score_ahead(h)
        for h in range(N_HEADS):
            st = sc_ref[rd, h]
            score_ahead(h + QK_LEAD)
            if diagonal:
                st = st + jnp.where(rk <= rq, 0.0, NEG)
            acc_ref, m_ref, l_ref = state_refs[h], state_refs[N_HEADS + h], state_refs[2 * N_HEADS + h]
            m_old = m_ref[...]
            m_new = jnp.maximum(m_old, jnp.max(st, axis=0, keepdims=True))
            alpha = jnp.exp2(m_old - m_new)
            p = jnp.exp2(st - m_new)
            m_ref[...] = m_new
            pv = jnp.dot(jnp.concatenate([vtj[h * hd:(h + 1) * hd, :], ones_rows], axis=0),
                         p.astype(BF16), preferred_element_type=F32)
            l_ref[...] = alpha * l_ref[...] + pv[hd:hd + 1, :]
            acc_ref[...] = alpha * acc_ref[...] + pv[0:hd, :]

    kb0 = kb_ref[0]
    e0 = key_aug(0)
    for h in range(N_HEADS):
        sc_ref[0, h] = scores(kb0, e0, h)
        state_refs[h][...] = jnp.zeros((hd, blk), F32)
        state_refs[N_HEADS + h][...] = jnp.full((1, blk), -jnp.inf, F32)
        state_refs[2 * N_HEADS + h][...] = jnp.zeros((1, blk), F32)

    def pair(t, carry):
        block(2 * t, 2 * t + 1, False, 0, 1)
        block(2 * t + 1, 2 * t + 2, False, 1, 0)
        return carry

    lax.fori_loop(0, i // 2, pair, 0)

    def finish(rd):
        block(i, None, True, rd, None)
        o_t = jnp.concatenate(
            [state_refs[h][...] / state_refs[2 * N_HEADS + h][...] for h in range(N_HEADS)], axis=0)
        o_ref[...] = jnp.transpose(o_t)

    @pl.when(i % 2 == 0)
    def _():
        finish(0)

    @pl.when(i % 2 == 1)
    def _():
        block(i - 1, i, False, 0, 1)
        finish(1)


def _attn_prompt(qt, st, kb, vt, cst):
    b, nb, blk, a = kb.shape
    assert blk == MOBA_BLOCK and nb <= AUG_ROWS and a == ATTN_WIDTH and st.shape[1] == N_HEADS * AUG_ROWS
    return pl.pallas_call(
        _attn_prompt_kernel,
        grid=(b, nb),
        in_specs=[pl.BlockSpec(cst.shape, lambda bi, i: (0, 0, 0)),
                  pl.BlockSpec((None, a, blk), lambda bi, i: (bi, 0, i)),
                  pl.BlockSpec((None, N_HEADS * AUG_ROWS, blk), lambda bi, i: (bi, 0, i)),
                  pl.BlockSpec((None, nb, blk, a), lambda bi, i: (bi, 0, 0, 0)),
                  pl.BlockSpec((None, nb, a, blk), lambda bi, i: (bi, 0, 0, 0))],
        out_specs=pl.BlockSpec((None, blk, a), lambda bi, i: (bi, i, 0)),
        out_shape=jax.ShapeDtypeStruct((b, nb * blk, a), F32),
        scratch_shapes=([pltpu.VMEM((N_HEADS, 2 * LANE, blk), BF16),
                         pltpu.VMEM((2, N_HEADS, blk, blk), F32)]
                        + [pltpu.VMEM((HEAD_DIM, blk), F32)] * N_HEADS
                        + [pltpu.VMEM((1, blk), F32)] * (2 * N_HEADS)),
        compiler_params=pltpu.CompilerParams(
            dimension_semantics=("arbitrary", "arbitrary"), vmem_limit_bytes=VMEM_LIMIT),
        name="attn_prompt",
    )(cst, qt, st, kb, vt)


class _NoSide:
    def region(self):
        pass

    def close(self):
        pass


MXU_TILE = 2 * LANE


def _ffn_body(x_ref, a_ref, cn_ref, gao_ref, wo_ref, gffn_ref, wg_ref, wu_ref, wd_ref, gfin_ref,
              y_ref, side):
    ff = wg_ref.shape[1]
    split = (ff // 2 + MXU_TILE - 1) // MXU_TILE * MXU_TILE
    side.region()
    an = _rms(a_ref[...], gao_ref[...]).astype(BF16)
    mixed = jnp.concatenate([an, cn_ref[...]], axis=1)
    x1 = x_ref[...] + jnp.dot(mixed, wo_ref[...], preferred_element_type=F32)
    h2 = _rms(x1, gffn_ref[...]).astype(BF16)
    x2 = x1
    for lo, hi in ((0, split), (split, ff)):
        side.region()
        gate = jnp.dot(h2, wg_ref[:, lo:hi], preferred_element_type=F32)
        side.region()
        up = jnp.dot(h2, wu_ref[:, lo:hi], preferred_element_type=F32)
        act = (gate * (1.0 / (1.0 + jnp.exp(-gate))) * up).astype(BF16)
        side.region()
        x2 = x2 + jnp.dot(act, wd_ref[lo:hi, :], preferred_element_type=F32)
    side.region()
    y_ref[...] = _rms(x2, gfin_ref[...])
    side.close()


N_FFN_PARTS = 8
N_RING = 64


class _GateStream:
    def __init__(self, b, pt_ref, q_ref, ck_ref, idx_ref, buf_ref, sem_ref, part_ref):
        self.b, self.pt_ref, self.q_ref, self.ck_ref = b, pt_ref, q_ref, ck_ref
        self.idx_ref, self.buf_ref, self.sem_ref, self.part_ref = idx_ref, buf_ref, sem_ref, part_ref
        self.n_seq, self.n_pages = pt_ref.shape
        self.per_slice = self.n_pages // N_FFN_PARTS
        self.k = -1

        @pl.when(b == 0)
        def _():
            for page in range(N_RING):
                self._copy(0, page).start()

    def _copy(self, seq, page):
        return pltpu.make_async_copy(self.ck_ref.at[0, self.pt_ref[seq, page]],
                                     self.buf_ref.at[page % N_RING], self.sem_ref.at[page % N_RING])

    def _pages(self):
        return range(self.k * self.per_slice, (self.k + 1) * self.per_slice)

    def region(self):
        self.k += 1
        pages = self._pages()
        for page in pages:
            self._copy(self.b, page).wait()
        self._reduce(range(pages.start // PAGES_PER_BLOCK, pages.stop // PAGES_PER_BLOCK))
        self._refill(pages)

    def _reduce(self, blocks):
        for h in range(N_HEADS):
            qh = self.q_ref[h]
            for n in blocks:
                ksum = None
                for half in range(PAGES_PER_BLOCK):
                    page = self.buf_ref[(n * PAGES_PER_BLOCK + half) % N_RING, h]
                    ksum = page if ksum is None else ksum + page
                prod = ksum * qh
                self.part_ref[n, h] = jnp.sum(
                    prod.reshape(HEAD_DIM // SUBLANE, SUBLANE, PAGE_SIZE), axis=0)

    def _refill(self, pages):
        if pages.stop + N_RING <= self.n_pages:
            for page in pages:
                self._copy(self.b, page + N_RING).start()
        else:

            @pl.when(self.b + 1 < self.n_seq)
            def _():
                for page in pages:
                    self._copy(self.b + 1, page + N_RING - self.n_pages).start()

    def close(self):
        assert self.k == N_FFN_PARTS - 1
        nblk = self.n_pages // PAGES_PER_BLOCK
        g = jnp.sum(self.part_ref[...], axis=(2, 3))
        nidx = lax.broadcasted_iota(jnp.int32, g.shape, 0)
        picks = []
        for _ in range(MOBA_TOPK):
            mx = jnp.max(g, axis=0, keepdims=True)
            pick = jnp.min(jnp.where(g == mx, nidx, nblk), axis=0, keepdims=True)
            picks.append(pick)
            g = jnp.where(nidx == pick, -jnp.inf, g)
        self.idx_ref[...] = jnp.concatenate(picks, axis=0)


def _ffn_kernel(*refs, with_gate):
    if with_gate:
        pt_ref, q_ref, ck_ref = refs[0], refs[11], refs[12]
        ffn_refs = refs[1:11] + refs[13:14]
        side = _GateStream(pl.program_id(0), pt_ref, q_ref, ck_ref, *refs[14:18])
    else:
        ffn_refs = refs
        side = _NoSide()
    _ffn_body(*ffn_refs, side)


def _ffn(x, attn, cn, gao, wo, gffn, wg, wu, wd, gfin, tm, gate_inputs=None):
    n, d = x.shape
    a = attn.shape[1]
    c = cn.shape[1]
    assert n % tm == 0
    row = lambda w: pl.BlockSpec((tm, w), lambda i, *_: (i, 0))
    res = lambda arr: pl.BlockSpec(arr.shape, lambda i, *_: (0, 0), pipeline_mode=pl.Buffered(1))
    in_specs = [row(d), row(a), row(c), res(gao), res(wo), res(gffn), res(wg), res(wu), res(wd),
                res(gfin)]
    out_specs = [row(d)]
    out_shape = [jax.ShapeDtypeStruct((n, d), F32)]
    scratch = []
    args = (x, attn, cn, gao, wo, gffn, wg, wu, wd, gfin)
    if gate_inputs is not None:
        page_table, q_bc, cache_kt = gate_inputs
        n_seq, n_pages = page_table.shape
        nblk = n_pages // PAGES_PER_BLOCK
        per_slice = n_pages // N_FFN_PARTS
        assert n // tm == n_seq and n_pages % N_FFN_PARTS == 0 and per_slice % PAGES_PER_BLOCK == 0
        assert N_RING % per_slice == 0 and n_pages % N_RING == 0 and nblk >= MOBA_TOPK
        assert cache_kt.shape[2:] == (N_HEADS, HEAD_DIM, PAGE_SIZE) and HEAD_DIM % SUBLANE == 0
        in_specs += [pl.BlockSpec((None, N_HEADS, HEAD_DIM, PAGE_SIZE), lambda i, *_: (i, 0, 0, 0)),
                     pl.BlockSpec(memory_space=pl.ANY)]
        out_specs += [pl.BlockSpec((None, MOBA_TOPK, N_HEADS), lambda i, *_: (i, 0, 0))]
        out_shape += [jax.ShapeDtypeStruct((n_seq, MOBA_TOPK, N_HEADS), jnp.int32)]
        scratch = [pltpu.VMEM((N_RING, N_HEADS, HEAD_DIM, PAGE_SIZE), F32),
                   pltpu.SemaphoreType.DMA((N_RING,)),
                   pltpu.VMEM((nblk, N_HEADS, SUBLANE, PAGE_SIZE), F32)]
        args = (page_table,) + args + (q_bc, cache_kt)
    grid_spec = pltpu.PrefetchScalarGridSpec(
        num_scalar_prefetch=0 if gate_inputs is None else 1,
        grid=(n // tm,), in_specs=in_specs, out_specs=out_specs, scratch_shapes=scratch)
    out = pl.pallas_call(
        functools.partial(_ffn_kernel, with_gate=gate_inputs is not None),
        grid_spec=grid_spec,
        out_shape=out_shape,
        compiler_params=pltpu.CompilerParams(dimension_semantics=("arbitrary",),
                                             vmem_limit_bytes=VMEM_LIMIT),
        name="ffn_%d" % n,
    )(*args)
    return out[0] if gate_inputs is None else out


def _mix_sample_kernel(x_ref, g_ref, w_ref, wc_ref, gco_ref, h0_ref, h1_ref,
                       q_ref, k_ref, v_ref, cn_ref, xc_ref):
    a = ATTN_WIDTH
    c = cn_ref.shape[1]
    h = _rms(x_ref[...], g_ref[...]).astype(BF16)
    z = jnp.dot(h, w_ref[...], preferred_element_type=F32)
    q_ref[...] = z[:, 0:a]
    k_ref[...] = z[:, a:2 * a]
    v_ref[...] = z[:, 2 * a:3 * a]
    gb = z[:, 3 * a:3 * a + c]
    xc = z[:, 3 * a + c:3 * a + 2 * c] * z[:, 3 * a + 2 * c:3 * a + 3 * c]
    y = wc_ref[0:1, :] * h0_ref[...] + wc_ref[1:2, :] * h1_ref[...] + wc_ref[2:3, :] * xc
    cn_ref[...] = _rms(gb * y, gco_ref[...]).astype(BF16)
    xc_ref[...] = xc


def _mix_sample(x, g, w_bf, wc, gco, h0, h1):
    n = x.shape[0]
    a = ATTN_WIDTH
    c = wc.shape[1]
    return pl.pallas_call(
        _mix_sample_kernel,
        out_shape=[jax.ShapeDtypeStruct((n, a), F32)] * 3
        + [jax.ShapeDtypeStruct((n, c), BF16), jax.ShapeDtypeStruct((n, c), F32)],
        compiler_params=pltpu.CompilerParams(vmem_limit_bytes=VMEM_LIMIT),
        name="mix_sample",
    )(x, g, w_bf, wc, gco, h0, h1)


def _attn_sample_kernel(pt_ref, idx_ref, q_ref, kn_ref, vn_ref, ck_ref, cv_ref, o_ref,
                        kbuf_ref, vbuf_ref, sem_ref, *, past_len, slopes):
    b = pl.program_id(0)
    n_seq = pl.num_programs(0)
    n_slab = MOBA_TOPK * PAGES_PER_BLOCK
    slot = b % 2

    def copies(seq, sl):
        cps = []
        for h in range(N_HEADS):
            for t in range(MOBA_TOPK):
                blk = idx_ref[seq, t, h]
                for half in range(PAGES_PER_BLOCK):
                    pg = pt_ref[seq, blk * PAGES_PER_BLOCK + half]
                    u = t * PAGES_PER_BLOCK + half
                    cps.append(pltpu.make_async_copy(ck_ref.at[0, pg, h], kbuf_ref.at[sl, h, u],
                                                     sem_ref.at[0, sl]))
                    cps.append(pltpu.make_async_copy(cv_ref.at[0, pg, h], vbuf_ref.at[sl, h, u],
                                                     sem_ref.at[1, sl]))
        return cps

    @pl.when(b == 0)
    def _():
        for cp in copies(0, 0):
            cp.start()

    @pl.when(b + 1 < n_seq)
    def _():
        for cp in copies(b + 1, 1 - slot):
            cp.start()

    for cp in copies(b, slot):
        cp.wait()

    pos = lax.broadcasted_iota(jnp.int32, (1, PAGE_SIZE), 1)
    for h in range(N_HEADS):
        qh = q_ref[h] * ATTN_SCALE
        own = jnp.sum(qh * kn_ref[h], axis=0, keepdims=True)
        rows = []
        for t in range(MOBA_TOPK):
            blk = idx_ref[b, t, h]
            for half in range(PAGES_PER_BLOCK):
                u = t * PAGES_PER_BLOCK + half
                kpos = blk * MOBA_BLOCK + half * PAGE_SIZE + pos
                dist = (past_len - kpos).astype(F32)
                rows.append(jnp.sum(kbuf_ref[slot, h, u] * qh, axis=0, keepdims=True)
                            - float(slopes[h]) * dist)
        mrow = functools.reduce(jnp.maximum, rows)
        m = jnp.maximum(jnp.max(mrow, axis=1, keepdims=True), own)
        p_own = jnp.exp(own - m)
        den = p_own
        acc = jnp.zeros((HEAD_DIM, PAGE_SIZE), F32)
        for u in range(n_slab):
            p = jnp.exp(rows[u] - m)
            den = den + jnp.sum(p, axis=1, keepdims=True)
            acc = acc + vbuf_ref[slot, h, u] * p
        num = jnp.sum(acc, axis=1, keepdims=True) + (p_own * vn_ref[h])[:, 0:1]
        o_ref[:, h:h + 1] = num / den[:, 0:1]


def _attn_sample(page_table, idx, q_bc, kn_bc, vn_bc, cache_kt, cache_vt, past_len):
    n_seq = q_bc.shape[0]
    blk4 = pl.BlockSpec((None, N_HEADS, HEAD_DIM, PAGE_SIZE), lambda i, pt, ix: (i, 0, 0, 0))
    n_slab = MOBA_TOPK * PAGES_PER_BLOCK
    grid_spec = pltpu.PrefetchScalarGridSpec(
        num_scalar_prefetch=2,
        grid=(n_seq,),
        in_specs=[blk4, blk4, blk4, pl.BlockSpec(memory_space=pl.ANY), pl.BlockSpec(memory_space=pl.ANY)],
        out_specs=pl.BlockSpec((None, HEAD_DIM, N_HEADS), lambda i, pt, ix: (i, 0, 0)),
        scratch_shapes=[pltpu.VMEM((2, N_HEADS, n_slab, HEAD_DIM, PAGE_SIZE), F32),
                        pltpu.VMEM((2, N_HEADS, n_slab, HEAD_DIM, PAGE_SIZE), F32),
                        pltpu.SemaphoreType.DMA((2, 2))])
    kern = functools.partial(_attn_sample_kernel, past_len=past_len, slopes=_alibi_slopes_np())
    return pl.pallas_call(
        kern,
        grid_spec=grid_spec,
        out_shape=jax.ShapeDtypeStruct((n_seq, HEAD_DIM, N_HEADS), F32),
        compiler_params=pltpu.CompilerParams(dimension_semantics=("arbitrary",),
                                             vmem_limit_bytes=VMEM_LIMIT),
        name="attn_sample",
    )(page_table, idx, q_bc, kn_bc, vn_bc, cache_kt, cache_vt)


def kernel(x_prompt, x_sample, cache_k, cache_v, state_conv, page_table, norm_mix, w_mix, w_conv,
           norm_attn_out, norm_conv_out, w_o, norm_ffn, w_gate, w_up, w_down, norm_final):
    depth = w_mix.shape[0]
    assert depth == 1 and x_sample.shape[1] == 1
    b, s, d = x_prompt.shape
    n_seq = x_sample.shape[0]
    n_pages = page_table.shape[1]
    past_len = n_pages * PAGE_SIZE
    a = ATTN_WIDTH
    c = w_conv.shape[2]

    g_mix = norm_mix[0][None, :]
    g_ao = norm_attn_out[0][None, :]
    g_co = norm_conv_out[0][None, :]
    g_ffn = norm_ffn[0][None, :]
    g_fin = norm_final[None, :]
    w_mix_bf = w_mix[0].astype(BF16)
    w_o_bf = w_o[0].astype(BF16)
    w_gate_bf = w_gate[0].astype(BF16)
    w_up_bf = w_up[0].astype(BF16)
    w_down_bf = w_down[0].astype(BF16)
    wc = w_conv[0]

    sl = (_alibi_slopes_np().astype(np.float64) * LOG2E).astype(np.float32)
    hi = sl.astype(jnp.bfloat16).astype(np.float32)
    mid = (sl - hi).astype(jnp.bfloat16).astype(np.float32)
    lo = (sl - hi - mid).astype(jnp.bfloat16).astype(np.float32)
    cst_np = np.zeros((N_HEADS, AUG_ROWS, MOBA_BLOCK), np.float32)
    for r, piece in enumerate((hi, mid, lo, hi, mid, lo)):
        cst_np[:, r, :] = piece[:, None]
    cst = jnp.asarray(cst_np, dtype=BF16)

    xs = x_sample[:, 0, :]
    h0 = state_conv[0, :, 0, :]
    h1 = state_conv[0, :, 1, :]
    q_s, k_s, v_s, cn_s, xc_s = _mix_sample(xs, g_mix, w_mix_bf, wc, g_co, h0, h1)
    hs = (N_HEADS, HEAD_DIM)
    lane_bc = lambda t: jnp.broadcast_to(t.reshape((n_seq,) + hs + (1,)), (n_seq,) + hs + (PAGE_SIZE,))
    q_bc, kn_bc, vn_bc = lane_bc(q_s), lane_bc(k_s), lane_bc(v_s)
    cache_kt = jnp.transpose(cache_k, (0, 1, 3, 4, 2))
    cache_vt = jnp.transpose(cache_v, (0, 1, 3, 4, 2))

    k_p, v_p, kb_p, vt_p, qt_p, st_p, cn_p, cst_p = _mix_prompt(x_prompt, g_mix, w_mix_bf, wc, g_co)
    attn_p = _attn_prompt(qt_p, st_p, kb_p.reshape(b, s // MOBA_BLOCK, MOBA_BLOCK, a), vt_p, cst)
    y_p, idx = _ffn(x_prompt.reshape(b * s, d), attn_p.reshape(b * s, a), cn_p.reshape(b * s, c),
                    g_ao, w_o_bf, g_ffn, w_gate_bf, w_up_bf, w_down_bf, g_fin, tm=b * s // n_seq,
                    gate_inputs=(page_table, q_bc, cache_kt))
    y_p = y_p.reshape(b, s, d)

    attn_s = _attn_sample(page_table, idx, q_bc, kn_bc, vn_bc, cache_kt, cache_vt, past_len)
    attn_s = jnp.transpose(attn_s, (0, 2, 1)).reshape(n_seq, a)
    y_s = _ffn(xs, attn_s, cn_s, g_ao, w_o_bf, g_ffn, w_gate_bf, w_up_bf, w_down_bf, g_fin, tm=n_seq)

    return (y_p, y_s[:, None, :],
            k_p.reshape((1, b, s) + hs), v_p.reshape((1, b, s) + hs), cst_p[None],
            k_s.reshape((1, n_seq, 1) + hs), v_s.reshape((1, n_seq, 1) + hs),
            jnp.stack([h1, xc_s], axis=1)[None])
```

```python
import functools

import numpy as np
import jax
import jax.numpy as jnp
from jax import lax
from jax.experimental import pallas as pl
from jax.experimental.pallas import tpu as pltpu

F32 = jnp.float32
BF16 = jnp.bfloat16

HEAD_DIM = 64
N_HEADS = 8
ATTN_WIDTH = N_HEADS * HEAD_DIM
MOBA_BLOCK = 256
MOBA_TOPK = 3
PAGE_SIZE = 128
PAGES_PER_BLOCK = MOBA_BLOCK // PAGE_SIZE
RMS_EPS = 1e-6
ATTN_SCALE = HEAD_DIM ** -0.5
LOG2E = 1.4426950408889634
NEG = -1e30
LANE = 128
SUBLANE = 8
VMEM_LIMIT = 56 * 1024 * 1024


def _alibi_slopes_np():
    return (2.0 ** (-8.0 * np.arange(1, N_HEADS + 1, dtype=np.float64) / N_HEADS)).astype(np.float32)


def _rms(x, g):
    y = x * lax.rsqrt(jnp.mean(x * x, axis=-1, keepdims=True) + RMS_EPS)
    return y * g


def _split_bf16(x):
    hi = x.astype(BF16)
    lo = (x - hi.astype(F32)).astype(BF16)
    return hi, lo


_DN_T = (((1,), (1,)), ((), ()))


MIX_CHUNK = 512


def _mix_project(x_ref, g_ref, w_ref, z_ref, finish=()):
    h = _rms(x_ref[...], g_ref[...]).astype(BF16)
    finish = iter(finish)
    for c0 in range(0, w_ref.shape[1], MIX_CHUNK):
        z_ref[:, c0:c0 + MIX_CHUNK] = jnp.dot(h, w_ref[:, c0:c0 + MIX_CHUNK],
                                              preferred_element_type=F32)
        next(finish, None)
    for _ in finish:
        pass


def _mix_finish(s, z_ref, wc_ref, gco_ref,
                k_ref, v_ref, kb_ref, vt_ref, qt_ref, st_ref, cn_ref, cst_ref, carry_ref, kmean_ref):
    tm = z_ref.shape[0]
    a = ATTN_WIDTH
    c = cn_ref.shape[1]
    first = s == 0

    k = z_ref[:, a:2 * a]
    v = z_ref[:, 2 * a:3 * a]
    k_ref[...] = k
    v_ref[...] = v
    kb_ref[...] = k.astype(BF16)
    vt = jnp.transpose(v)
    vt_ref[...] = vt.astype(BF16)
    ksum = jnp.sum(k, axis=0, keepdims=True) * (1.0 / MOBA_BLOCK)
    yield

    qt = jnp.transpose(z_ref[:, 0:a])
    qt_ref[...] = (qt * (ATTN_SCALE * LOG2E)).astype(BF16)
    yield

    gb = z_ref[:, 3 * a:3 * a + c]
    gc = z_ref[:, 3 * a + c:3 * a + 2 * c]
    u = z_ref[:, 3 * a + 2 * c:3 * a + 3 * c]
    xc = gc * u
    row = lax.broadcasted_iota(jnp.int32, xc.shape, 0)
    p0 = jnp.where(first, 0.0, carry_ref[0:1, :])
    p1 = jnp.where(first, 0.0, carry_ref[1:2, :])
    xc1 = jnp.where(row == 0, p1, pltpu.roll(xc, 1, axis=0))
    xc2 = jnp.where(row == 0, p0, jnp.where(row == 1, p1, pltpu.roll(xc, 2, axis=0)))
    y = wc_ref[0:1, :] * xc2 + wc_ref[1:2, :] * xc1 + wc_ref[2:3, :] * xc
    cn = _rms(gb * y, gco_ref[...])
    cn_ref[...] = cn.astype(BF16)
    last2 = xc[tm - 2:tm, :]
    carry_ref[...] = last2
    cst_ref[...] = last2
    yield

    nb = kmean_ref.shape[0]
    hmask = (lax.broadcasted_iota(jnp.int32, (N_HEADS, a), 1) // HEAD_DIM
             == lax.broadcasted_iota(jnp.int32, (N_HEADS, a), 0))
    kmt = jnp.where(hmask[None], kmean_ref[...][:, None, :], 0.0).reshape(nb * N_HEADS, a)
    qh, ql = _split_bf16(qt)
    kh, kl = _split_bf16(kmt)
    gate = (jnp.dot(kh, qh, preferred_element_type=F32) + jnp.dot(kh, ql, preferred_element_type=F32)
            + jnp.dot(kl, qh, preferred_element_type=F32))
    yield

    gm = [jnp.where(j < s, gate[j * N_HEADS:(j + 1) * N_HEADS, :], -jnp.inf) for j in range(nb)]
    cnt = [jnp.zeros((N_HEADS, tm), F32) for _ in range(nb)]
    for j in range(nb):
        for m in range(j):
            m_beats_j = gm[m] >= gm[j]
            cnt[j] = cnt[j] + jnp.where(m_beats_j, 1.0, 0.0)
            cnt[m] = cnt[m] + jnp.where(m_beats_j, 0.0, 1.0)
        if j == nb // 2:
            yield
    sel = [jnp.where(j < s, jnp.where(cnt[j] < MOBA_TOPK, 0.0, NEG), jnp.where(j == s, 0.0, NEG))
           for j in range(nb)]
    st_ref[...] = jnp.concatenate([sel[j][h:h + 1, :] for h in range(N_HEADS) for j in range(nb)],
                                  axis=0).astype(BF16)

    kmean_ref[pl.ds(s, 1), :] = ksum


def _mix_prompt_kernel(x_ref, g_ref, w_ref, wc_ref, gco_ref,
                       k_ref, v_ref, kb_ref, vt_ref, qt_ref, st_ref, cn_ref, cst_ref,
                       carry_ref, kmean_ref, za_ref, zb_ref, *, tiles_per_seq):
    t = pl.program_id(0)
    s_prev = (t + tiles_per_seq - 1) % tiles_per_seq
    outs = (k_ref, v_ref, kb_ref, vt_ref, qt_ref, st_ref, cn_ref, cst_ref, carry_ref, kmean_ref)

    @pl.when(t == 0)
    def _():
        carry_ref[...] = jnp.zeros_like(carry_ref)
        kmean_ref[...] = jnp.zeros_like(kmean_ref)
        _mix_project(x_ref, g_ref, w_ref, za_ref)

    @pl.when(t % 2 == 1)
    def _():
        _mix_project(x_ref, g_ref, w_ref, zb_ref, _mix_finish(s_prev, za_ref, wc_ref, gco_ref, *outs))

    @pl.when((t % 2 == 0) & (t > 0))
    def _():
        _mix_project(x_ref, g_ref, w_ref, za_ref, _mix_finish(s_prev, zb_ref, wc_ref, gco_ref, *outs))


def _mix_prompt(x, g, w_bf, wc, gco):
    b, s, d = x.shape
    tm = MOBA_BLOCK
    ns = s // tm
    nt = b * ns
    a = ATTN_WIDTH
    c = wc.shape[1]
    assert s % tm == 0 and ns * N_HEADS == LANE and w_bf.shape[1] == 3 * a + 3 * c and nt % 2 == 0
    tin = lambda t: jnp.minimum(t, nt - 1)
    tout = lambda t: jnp.maximum(t - 1, 0)
    row_in = lambda n: pl.BlockSpec((None, tm, n), lambda t: (tin(t) // ns, tin(t) % ns, 0))
    row_spec = lambda n: pl.BlockSpec((None, tm, n), lambda t: (tout(t) // ns, tout(t) % ns, 0))
    col_spec = lambda n: pl.BlockSpec((None, n, tm), lambda t: (tout(t) // ns, 0, tout(t) % ns))
    full = lambda shp: pl.BlockSpec(shp, lambda t: (0,) * len(shp))
    return pl.pallas_call(
        functools.partial(_mix_prompt_kernel, tiles_per_seq=ns),
        grid=(nt + 1,),
        in_specs=[row_in(d), full((1, d)), full(w_bf.shape), full(wc.shape), full((1, c))],
        out_specs=[row_spec(a), row_spec(a), row_spec(a),
                   pl.BlockSpec((None, None, a, tm), lambda t: (tout(t) // ns, tout(t) % ns, 0, 0)),
                   col_spec(a), col_spec(LANE), row_spec(c),
                   pl.BlockSpec((None, 2, c), lambda t: (tout(t) // ns, 0, 0))],
        out_shape=[jax.ShapeDtypeStruct((b, s, a), F32),
                   jax.ShapeDtypeStruct((b, s, a), F32),
                   jax.ShapeDtypeStruct((b, s, a), BF16),
                   jax.ShapeDtypeStruct((b, ns, a, tm), BF16),
                   jax.ShapeDtypeStruct((b, a, s), BF16),
                   jax.ShapeDtypeStruct((b, LANE, s), BF16),
                   jax.ShapeDtypeStruct((b, s, c), BF16),
                   jax.ShapeDtypeStruct((b, 2, c), F32)],
        scratch_shapes=[pltpu.VMEM((2, c), F32), pltpu.VMEM((ns, a), F32),
                        pltpu.VMEM((tm, w_bf.shape[1]), F32), pltpu.VMEM((tm, w_bf.shape[1]), F32)],
        compiler_params=pltpu.CompilerParams(dimension_semantics=("arbitrary",),
                                             vmem_limit_bytes=VMEM_LIMIT),
        name="mix_prompt",
    )(x, g, w_bf, wc, gco)


AUG_ROWS = 16


QK_LEAD = 2


def _attn_prompt_kernel(cst_ref, qt_ref, st_ref, kb_ref, vt_ref, o_ref, w_ref, sc_ref, *state_refs):
    i = pl.program_id(1)
    blk = MOBA_BLOCK
    hd = HEAD_DIM

    zq = jnp.zeros((hd, blk), BF16)
    zpad = jnp.zeros((LANE - 2 * AUG_ROWS, blk), BF16)
    for h in range(N_HEADS):
        qh = qt_ref[h * hd:(h + 1) * hd, :]
        qpart = [qh, zq] if h % 2 == 0 else [zq, qh]
        w_ref[h] = jnp.concatenate(
            qpart + [st_ref[h * AUG_ROWS:(h + 1) * AUG_ROWS, :], cst_ref[h], zpad], axis=0)

    elane = lax.broadcasted_iota(jnp.int32, (blk, LANE), 1)
    erow = lax.broadcasted_iota(jnp.int32, (blk, LANE), 0).astype(F32)
    ebase = jnp.where((elane >= AUG_ROWS) & (elane < AUG_ROWS + 3), erow, 0.0)
    rk = lax.broadcasted_iota(jnp.int32, (blk, blk), 0)
    rq = lax.broadcasted_iota(jnp.int32, (blk, blk), 1)
    ones_rows = jnp.ones((2 * SUBLANE, blk), BF16)

    def key_aug(j):
        off = ((j - i) * blk).astype(F32)
        return jnp.where(elane == j, 1.0,
                         jnp.where((elane >= AUG_ROWS + 3) & (elane < AUG_ROWS + 6), off, ebase)).astype(BF16)

    def scores(kbj, e, h):
        rhs = jnp.concatenate([kbj[:, (h // 2) * LANE:(h // 2 + 1) * LANE], e], axis=1)
        return jnp.dot(rhs, w_ref[h], preferred_element_type=F32)

    def block(j, nxt, diagonal, rd, wr):
        vtj = vt_ref[j]
        if nxt is not None:
            kbn = kb_ref[nxt]
            en = key_aug(nxt)

        def score_ahead(h):
            if nxt is not None and h < N_HEADS:
                sc_ref[wr, h] = scores(kbn, en, h)

        for h in range(QK_LEAD):
            score_ahead(h)
        for h in range(N_HEADS):
            st = sc_ref[rd, h]
            score_ahead(h + QK_LEAD)
            if diagonal:
                st = st + jnp.where(rk <= rq, 0.0, NEG)
            acc_ref, m_ref, l_ref = state_refs[h], state_refs[N_HEADS + h], state_refs[2 * N_HEADS + h]
            m_old = m_ref[...]
            m_new = jnp.maximum(m_old, jnp.max(st, axis=0, keepdims=True))
            alpha = jnp.exp2(m_old - m_new)
            p = jnp.exp2(st - m_new)
            m_ref[...] = m_new
            pv = jnp.dot(jnp.concatenate([vtj[h * hd:(h + 1) * hd, :], ones_rows], axis=0),
                         p.astype(BF16), preferred_element_type=F32)
            l_ref[...] = alpha * l_ref[...] + pv[hd:hd + 1, :]
            acc_ref[...] = alpha * acc_ref[...] + pv[0:hd, :]

    kb0 = kb_ref[0]
    e0 = key_aug(0)
    for h in range(N_HEADS):
        sc_ref[0, h] = scores(kb0, e0, h)
        state_refs[h][...] = jnp.zeros((hd, blk), F32)
        state_refs[N_HEADS + h][...] = jnp.full((1, blk), -jnp.inf, F32)
        state_refs[2 * N_HEADS + h][...] = jnp.zeros((1, blk), F32)

    def pair(t, carry):
        block(2 * t, 2 * t + 1, False, 0, 1)
        block(2 * t + 1, 2 * t + 2, False, 1, 0)
        return carry

    lax.fori_loop(0, i // 2, pair, 0)

    def finish(rd):
        block(i, None, True, rd, None)
        o_t = jnp.concatenate(
            [state_refs[h][...] / state_refs[2 * N_HEADS + h][...] for h in range(N_HEADS)], axis=0)
        o_ref[...] = jnp.transpose(o_t)

    @pl.when(i % 2 == 0)
    def _():
        finish(0)

    @pl.when(i % 2 == 1)
    def _():
        block(i - 1, i, False, 0, 1)
        finish(1)


def _attn_prompt(qt, st, kb, vt, cst):
    b, nb, blk, a = kb.shape
    assert blk == MOBA_BLOCK and nb <= AUG_ROWS and a == ATTN_WIDTH and st.shape[1] == N_HEADS * AUG_ROWS
    return pl.pallas_call(
        _attn_prompt_kernel,
        grid=(b, nb),
        in_specs=[pl.BlockSpec(cst.shape, lambda bi, i: (0, 0, 0)),
                  pl.BlockSpec((None, a, blk), lambda bi, i: (bi, 0, i)),
                  pl.BlockSpec((None, N_HEADS * AUG_ROWS, blk), lambda bi, i: (bi, 0, i)),
                  pl.BlockSpec((None, nb, blk, a), lambda bi, i: (bi, 0, 0, 0)),
                  pl.BlockSpec((None, nb, a, blk), lambda bi, i: (bi, 0, 0, 0))],
        out_specs=pl.BlockSpec((None, blk, a), lambda bi, i: (bi, i, 0)),
        out_shape=jax.ShapeDtypeStruct((b, nb * blk, a), F32),
        scratch_shapes=([pltpu.VMEM((N_HEADS, 2 * LANE, blk), BF16),
                         pltpu.VMEM((2, N_HEADS, blk, blk), F32)]
                        + [pltpu.VMEM((HEAD_DIM, blk), F32)] * N_HEADS
                        + [pltpu.VMEM((1, blk), F32)] * (2 * N_HEADS)),
        compiler_params=pltpu.CompilerParams(
            dimension_semantics=("arbitrary", "arbitrary"), vmem_limit_bytes=VMEM_LIMIT),
        name="attn_prompt",
    )(cst, qt, st, kb, vt)


class _NoSide:
    def region(self, n_tiles):
        pass

    def piece(self):
        return None

    def close(self):
        pass


MXU_TILE = 2 * LANE
TOKEN_SHAPE = (2 * SUBLANE, LANE)


def _zero_from(token):
    bits = pltpu.bitcast(token, jnp.uint32)
    half = jnp.uint32(16)
    return pltpu.bitcast(lax.shift_right_logical(lax.shift_right_logical(bits, half), half), F32)


def _after(x, token):
    if token is None:
        return x
    zero = _zero_from(token).astype(x.dtype)
    tr, tc = token.shape
    m, k = x.shape
    assert m % 2 == 0 and k % 2 == 0 and m // 2 >= tr and k // 2 >= tc

    def patch_cols(rows):
        return jnp.concatenate([rows[:, 0:tc] + zero, rows[:, tc:k // 2],
                                rows[:, k // 2:k // 2 + tc] + zero, rows[:, k // 2 + tc:]], axis=1)

    return jnp.concatenate([patch_cols(x[0:tr, :]), x[tr:m // 2, :],
                            patch_cols(x[m // 2:m // 2 + tr, :]), x[m // 2 + tr:, :]], axis=0)


def _ffn_body(x_ref, a_ref, cn_ref, gao_ref, wo_ref, gffn_ref, wg_ref, wu_ref, wd_ref, gfin_ref,
              y_ref, side):
    ff = wg_ref.shape[1]
    split = (ff // 2 + MXU_TILE - 1) // MXU_TILE * MXU_TILE

    def mm(lhs, w_ref, rows, cols):
        starts = list(range(cols.start, cols.stop, MXU_TILE))
        side.region(len(starts))
        tiles, token = [], None
        for c0 in starts:
            c1 = min(c0 + MXU_TILE, cols.stop)
            tiles.append(jnp.dot(_after(lhs, token), w_ref[rows, c0:c1], preferred_element_type=F32))
            token = side.piece()
        return jnp.concatenate(tiles, axis=1)

    d = x_ref.shape[1]
    an = _rms(a_ref[...], gao_ref[...]).astype(BF16)
    mixed = jnp.concatenate([an, cn_ref[...]], axis=1)
    x1 = x_ref[...] + mm(mixed, wo_ref, slice(None), range(0, d))
    h2 = _rms(x1, gffn_ref[...]).astype(BF16)
    x2 = x1
    for lo, hi in ((0, split), (split, ff)):
        gate = mm(h2, wg_ref, slice(None), range(lo, hi))
        up = mm(h2, wu_ref, slice(None), range(lo, hi))
        act = (gate * (1.0 / (1.0 + jnp.exp(-gate))) * up).astype(BF16)
        x2 = x2 + mm(act, wd_ref, slice(lo, hi), range(0, d))
    side.close()
    y_ref[...] = _rms(x2, gfin_ref[...])


N_FFN_DOTS = 7
N_RING = 64


def _split_even(items, parts):
    q, r = divmod(len(items), parts)
    sizes = [q + (1 if p < r else 0) for p in range(parts)]
    out, i = [], 0
    for n in sizes:
        if n:
            out.append(items[i:i + n])
        i += n
    return out


class _GateStream:
    def __init__(self, b, pt_ref, q_ref, ck_ref, idx_ref, buf_ref, sem_ref, part_ref):
        self.b, self.pt_ref, self.q_ref, self.ck_ref = b, pt_ref, q_ref, ck_ref
        self.idx_ref, self.buf_ref, self.sem_ref, self.part_ref = idx_ref, buf_ref, sem_ref, part_ref
        self.n_seq, self.n_pages = pt_ref.shape
        self.slices = _split_even(list(range(self.n_pages // PAGES_PER_BLOCK)), N_FFN_DOTS)
        self.k = -1
        self.pending = []
        self.token = None

        @pl.when(b == 0)
        def _():
            for page in range(N_RING):
                self._copy(0, page).start()

    def _copy(self, seq, page):
        return pltpu.make_async_copy(self.ck_ref.at[0, self.pt_ref[seq, page]],
                                     self.buf_ref.at[page % N_RING], self.sem_ref.at[page % N_RING])

    def _pages(self):
        blocks = self.slices[self.k]
        return range(blocks[0] * PAGES_PER_BLOCK, (blocks[-1] + 1) * PAGES_PER_BLOCK)

    def _finish_slice(self):
        while self.pending:
            self.piece()
        if 0 <= self.k < len(self.slices):
            self._refill(self._pages())

    def region(self, n_tiles):
        self._finish_slice()
        self.k += 1
        self.token = None
        if self.k < len(self.slices):
            for page in self._pages():
                self._copy(self.b, page).wait()
            self.pending = _split_even(self.slices[self.k], max(1, n_tiles - 1))

    def piece(self):
        if not self.pending:
            return None
        blocks = self.pending.pop(0)
        last = []
        for h in range(N_HEADS):
            qh = self.q_ref[h]
            if self.token is not None:
                tr = TOKEN_SHAPE[0]
                qh = jnp.concatenate([qh[0:tr] + _zero_from(self.token), qh[tr:]], axis=0)
            for n in blocks:
                ksum = None
                for half in range(PAGES_PER_BLOCK):
                    page = self.buf_ref[(n * PAGES_PER_BLOCK + half) % N_RING, h]
                    ksum = page if ksum is None else ksum + page
                prod = ksum * qh
                red = jnp.sum(prod.reshape(HEAD_DIM // SUBLANE, SUBLANE, PAGE_SIZE), axis=0)
                self.part_ref[n, h] = red
                last = (last + [red])[-2:]
        self.token = jnp.concatenate(last, axis=0)
        return self.token

    def _refill(self, pages):
        same = [p + N_RING for p in pages if p + N_RING < self.n_pages]
        nxt = [p + N_RING - self.n_pages for p in pages if p + N_RING >= self.n_pages]
        for page in same:
            self._copy(self.b, page).start()
        if nxt:

            @pl.when(self.b + 1 < self.n_seq)
            def _():
                for page in nxt:
                    self._copy(self.b + 1, page).start()

    def close(self):
        while self.k < len(self.slices) - 1:
            self.region(1)
        self._finish_slice()
        self.k = len(self.slices)
        nblk = self.n_pages // PAGES_PER_BLOCK
        g = jnp.sum(self.part_ref[...], axis=(2, 3))
        nidx = lax.broadcasted_iota(jnp.int32, g.shape, 0)
        picks = []
        for _ in range(MOBA_TOPK):
            mx = jnp.max(g, axis=0, keepdims=True)
            pick = jnp.min(jnp.where(g == mx, nidx, nblk), axis=0, keepdims=True)
            picks.append(pick)
            g = jnp.where(nidx == pick, -jnp.inf, g)
        self.idx_ref[...] = jnp.concatenate(picks, axis=0)


def _ffn_kernel(*refs, with_gate):
    if with_gate:
        pt_ref, q_ref, ck_ref = refs[0], refs[11], refs[12]
        ffn_refs = refs[1:11] + refs[13:14]
        side = _GateStream(pl.program_id(0), pt_ref, q_ref, ck_ref, *refs[14:18])
    else:
        ffn_refs = refs
        side = _NoSide()
    _ffn_body(*ffn_refs, side)


def _ffn(x, attn, cn, gao, wo, gffn, wg, wu, wd, gfin, tm, gate_inputs=None):
    n, d = x.shape
    a = attn.shape[1]
    c = cn.shape[1]
    assert n % tm == 0
    row = lambda w: pl.BlockSpec((tm, w), lambda i, *_: (i, 0))
    res = lambda arr: pl.BlockSpec(arr.shape, lambda i, *_: (0, 0), pipeline_mode=pl.Buffered(1))
    in_specs = [row(d), row(a), row(c), res(gao), res(wo), res(gffn), res(wg), res(wu), res(wd),
                res(gfin)]
    out_specs = [row(d)]
    out_shape = [jax.ShapeDtypeStruct((n, d), F32)]
    scratch = []
    args = (x, attn, cn, gao, wo, gffn, wg, wu, wd, gfin)
    if gate_inputs is not None:
        page_table, q_bc, cache_kt = gate_inputs
        n_seq, n_pages = page_table.shape
        nblk = n_pages // PAGES_PER_BLOCK
        assert n // tm == n_seq and n_pages % PAGES_PER_BLOCK == 0 and nblk >= MOBA_TOPK
        assert n_pages % N_RING == 0 and 2 * PAGES_PER_BLOCK * -(-nblk // N_FFN_DOTS) <= N_RING
        assert cache_kt.shape[2:] == (N_HEADS, HEAD_DIM, PAGE_SIZE) and HEAD_DIM % SUBLANE == 0
        in_specs += [pl.BlockSpec((None, N_HEADS, HEAD_DIM, PAGE_SIZE), lambda i, *_: (i, 0, 0, 0)),
                     pl.BlockSpec(memory_space=pl.ANY)]
        out_specs += [pl.BlockSpec((None, MOBA_TOPK, N_HEADS), lambda i, *_: (i, 0, 0))]
        out_shape += [jax.ShapeDtypeStruct((n_seq, MOBA_TOPK, N_HEADS), jnp.int32)]
        scratch = [pltpu.VMEM((N_RING, N_HEADS, HEAD_DIM, PAGE_SIZE), F32),
                   pltpu.SemaphoreType.DMA((N_RING,)),
                   pltpu.VMEM((nblk, N_HEADS, SUBLANE, PAGE_SIZE), F32)]
        args = (page_table,) + args + (q_bc, cache_kt)
    grid_spec = pltpu.PrefetchScalarGridSpec(
        num_scalar_prefetch=0 if gate_inputs is None else 1,
        grid=(n // tm,), in_specs=in_specs, out_specs=out_specs, scratch_shapes=scratch)
    out = pl.pallas_call(
        functools.partial(_ffn_kernel, with_gate=gate_inputs is not None),
        grid_spec=grid_spec,
        out_shape=out_shape,
        compiler_params=pltpu.CompilerParams(dimension_semantics=("arbitrary",),
                                             vmem_limit_bytes=VMEM_LIMIT),
        name="ffn_%d" % n,
    )(*args)
    return out[0] if gate_inputs is None else out


def _mix_sample_kernel(x_ref, g_ref, w_ref, wq_ref, wc_ref, gco_ref, h0_ref, h1_ref,
                       q_ref, k_ref, v_ref, cn_ref, xc_ref):
    a = ATTN_WIDTH
    c = cn_ref.shape[1]
    hf = _rms(x_ref[...], g_ref[...])
    h = hf.astype(BF16)
    z = jnp.dot(h, w_ref[...], preferred_element_type=F32)
    hh, hl = _split_bf16(hf)
    wh, wl = _split_bf16(wq_ref[...])
    q_ref[...] = (jnp.dot(hh, wh, preferred_element_type=F32) + jnp.dot(hl, wh, preferred_element_type=F32)
                  + jnp.dot(hh, wl, preferred_element_type=F32))
    k_ref[...] = z[:, a:2 * a]
    v_ref[...] = z[:, 2 * a:3 * a]
    gb = z[:, 3 * a:3 * a + c]
    xc = z[:, 3 * a + c:3 * a + 2 * c] * z[:, 3 * a + 2 * c:3 * a + 3 * c]
    y = wc_ref[0:1, :] * h0_ref[...] + wc_ref[1:2, :] * h1_ref[...] + wc_ref[2:3, :] * xc
    cn_ref[...] = _rms(gb * y, gco_ref[...]).astype(BF16)
    xc_ref[...] = xc


def _mix_sample(x, g, w_bf, wq, wc, gco, h0, h1):
    n = x.shape[0]
    a = ATTN_WIDTH
    c = wc.shape[1]
    return pl.pallas_call(
        _mix_sample_kernel,
        out_shape=[jax.ShapeDtypeStruct((n, a), F32)] * 3
        + [jax.ShapeDtypeStruct((n, c), BF16), jax.ShapeDtypeStruct((n, c), F32)],
        compiler_params=pltpu.CompilerParams(vmem_limit_bytes=VMEM_LIMIT),
        name="mix_sample",
    )(x, g, w_bf, wq, wc, gco, h0, h1)


def _attn_sample_kernel(pt_ref, idx_ref, q_ref, kn_ref, vn_ref, ck_ref, cv_ref, o_ref,
                        kbuf_ref, vbuf_ref, sem_ref, *, past_len, slopes):
    b = pl.program_id(0)
    n_seq = pl.num_programs(0)
    n_slab = MOBA_TOPK * PAGES_PER_BLOCK
    slot = b % 2

    def copies(seq, sl):
        cps = []
        for h in range(N_HEADS):
            for t in range(MOBA_TOPK):
                blk = idx_ref[seq, t, h]
                for half in range(PAGES_PER_BLOCK):
                    pg = pt_ref[seq, blk * PAGES_PER_BLOCK + half]
                    u = t * PAGES_PER_BLOCK + half
                    cps.append(pltpu.make_async_copy(ck_ref.at[0, pg, h], kbuf_ref.at[sl, h, u],
                                                     sem_ref.at[0, sl]))
                    cps.append(pltpu.make_async_copy(cv_ref.at[0, pg, h], vbuf_ref.at[sl, h, u],
                                                     sem_ref.at[1, sl]))
        return cps

    @pl.when(b == 0)
    def _():
        for cp in copies(0, 0):
            cp.start()

    @pl.when(b + 1 < n_seq)
    def _():
        for cp in copies(b + 1, 1 - slot):
            cp.start()

    for cp in copies(b, slot):
        cp.wait()

    pos = lax.broadcasted_iota(jnp.int32, (1, PAGE_SIZE), 1)
    for h in range(N_HEADS):
        qh = q_ref[h] * ATTN_SCALE
        own = jnp.sum(qh * kn_ref[h], axis=0, keepdims=True)
        rows = []
        for t in range(MOBA_TOPK):
            blk = idx_ref[b, t, h]
            for half in range(PAGES_PER_BLOCK):
                u = t * PAGES_PER_BLOCK + half
                kpos = blk * MOBA_BLOCK + half * PAGE_SIZE + pos
                dist = (past_len - kpos).astype(F32)
                rows.append(jnp.sum(kbuf_ref[slot, h, u] * qh, axis=0, keepdims=True)
                            - float(slopes[h]) * dist)
        mrow = functools.reduce(jnp.maximum, rows)
        m = jnp.maximum(jnp.max(mrow, axis=1, keepdims=True), own)
        p_own = jnp.exp(own - m)
        den = p_own
        acc = jnp.zeros((HEAD_DIM, PAGE_SIZE), F32)
        for u in range(n_slab):
            p = jnp.exp(rows[u] - m)
            den = den + jnp.sum(p, axis=1, keepdims=True)
            acc = acc + vbuf_ref[slot, h, u] * p
        num = jnp.sum(acc, axis=1, keepdims=True) + (p_own * vn_ref[h])[:, 0:1]
        o_ref[:, h:h + 1] = num / den[:, 0:1]


def _attn_sample(page_table, idx, q_bc, kn_bc, vn_bc, cache_kt, cache_vt, past_len):
    n_seq = q_bc.shape[0]
    blk4 = pl.BlockSpec((None, N_HEADS, HEAD_DIM, PAGE_SIZE), lambda i, pt, ix: (i, 0, 0, 0))
    n_slab = MOBA_TOPK * PAGES_PER_BLOCK
    grid_spec = pltpu.PrefetchScalarGridSpec(
        num_scalar_prefetch=2,
        grid=(n_seq,),
        in_specs=[blk4, blk4, blk4, pl.BlockSpec(memory_space=pl.ANY), pl.BlockSpec(memory_space=pl.ANY)],
        out_specs=pl.BlockSpec((None, HEAD_DIM, N_HEADS), lambda i, pt, ix: (i, 0, 0)),
        scratch_shapes=[pltpu.VMEM((2, N_HEADS, n_slab, HEAD_DIM, PAGE_SIZE), F32),
                        pltpu.VMEM((2, N_HEADS, n_slab, HEAD_DIM, PAGE_SIZE), F32),
                        pltpu.SemaphoreType.DMA((2, 2))])
    kern = functools.partial(_attn_sample_kernel, past_len=past_len, slopes=_alibi_slopes_np())
    return pl.pallas_call(
        kern,
        grid_spec=grid_spec,
        out_shape=jax.ShapeDtypeStruct((n_seq, HEAD_DIM, N_HEADS), F32),
        compiler_params=pltpu.CompilerParams(dimension_semantics=("arbitrary",),
                                             vmem_limit_bytes=VMEM_LIMIT),
        name="attn_sample",
    )(page_table, idx, q_bc, kn_bc, vn_bc, cache_kt, cache_vt)


def kernel(x_prompt, x_sample, cache_k, cache_v, state_conv, page_table, norm_mix, w_mix, w_conv,
           norm_attn_out, norm_conv_out, w_o, norm_ffn, w_gate, w_up, w_down, norm_final):
    depth = w_mix.shape[0]
    assert depth == 1 and x_sample.shape[1] == 1
    b, s, d = x_prompt.shape
    n_seq = x_sample.shape[0]
    n_pages = page_table.shape[1]
    past_len = n_pages * PAGE_SIZE
    a = ATTN_WIDTH
    c = w_conv.shape[2]

    g_mix = norm_mix[0][None, :]
    g_ao = norm_attn_out[0][None, :]
    g_co = norm_conv_out[0][None, :]
    g_ffn = norm_ffn[0][None, :]
    g_fin = norm_final[None, :]
    w_mix_bf = w_mix[0].astype(BF16)
    w_o_bf = w_o[0].astype(BF16)
    w_gate_bf = w_gate[0].astype(BF16)
    w_up_bf = w_up[0].astype(BF16)
    w_down_bf = w_down[0].astype(BF16)
    wc = w_conv[0]

    sl = (_alibi_slopes_np().astype(np.float64) * LOG2E).astype(np.float32)
    hi = sl.astype(jnp.bfloat16).astype(np.float32)
    mid = (sl - hi).astype(jnp.bfloat16).astype(np.float32)
    lo = (sl - hi - mid).astype(jnp.bfloat16).astype(np.float32)
    cst_np = np.zeros((N_HEADS, AUG_ROWS, MOBA_BLOCK), np.float32)
    for r, piece in enumerate((hi, mid, lo, hi, mid, lo)):
        cst_np[:, r, :] = piece[:, None]
    cst = jnp.asarray(cst_np, dtype=BF16)

    xs = x_sample[:, 0, :]
    h0 = state_conv[0, :, 0, :]
    h1 = state_conv[0, :, 1, :]
    q_s, k_s, v_s, cn_s, xc_s = _mix_sample(xs, g_mix, w_mix_bf, w_mix[0][:, :a], wc, g_co, h0, h1)
    hs = (N_HEADS, HEAD_DIM)
    lane_bc = lambda t: jnp.broadcast_to(t.reshape((n_seq,) + hs + (1,)), (n_seq,) + hs + (PAGE_SIZE,))
    q_bc, kn_bc, vn_bc = lane_bc(q_s), lane_bc(k_s), lane_bc(v_s)
    cache_kt = jnp.transpose(cache_k, (0, 1, 3, 4, 2))
    cache_vt = jnp.transpose(cache_v, (0, 1, 3, 4, 2))

    k_p, v_p, kb_p, vt_p, qt_p, st_p, cn_p, cst_p = _mix_prompt(x_prompt, g_mix, w_mix_bf, wc, g_co)
    attn_p = _attn_prompt(qt_p, st_p, kb_p.reshape(b, s // MOBA_BLOCK, MOBA_BLOCK, a), vt_p, cst)
    y_p, idx = _ffn(x_prompt.reshape(b * s, d), attn_p.reshape(b * s, a), cn_p.reshape(b * s, c),
                    g_ao, w_o_bf, g_ffn, w_gate_bf, w_up_bf, w_down_bf, g_fin, tm=b * s // n_seq,
                    gate_inputs=(page_table, q_bc, cache_kt))
    y_p = y_p.reshape(b, s, d)

    attn_s = _attn_sample(page_table, idx, q_bc, kn_bc, vn_bc, cache_kt, cache_vt, past_len)
    attn_s = jnp.transpose(attn_s, (0, 2, 1)).reshape(n_seq, a)
    y_s = _ffn(xs, attn_s, cn_s, g_ao, w_o_bf, g_ffn, w_gate_bf, w_up_bf, w_down_bf, g_fin, tm=n_seq)

    return (y_p, y_s[:, None, :],
            k_p.reshape((1, b, s) + hs), v_p.reshape((1, b, s) + hs), cst_p[None],
            k_s.reshape((1, n_seq, 1) + hs), v_s.reshape((1, n_seq, 1) + hs),
            jnp.stack([h1, xc_s], axis=1)[None])
```

```python
import functools

import numpy as np
import jax
import jax.numpy as jnp
from jax import lax
from jax.experimental import pallas as pl
from jax.experimental.pallas import tpu as pltpu

F32 = jnp.float32
BF16 = jnp.bfloat16

HEAD_DIM = 64
N_HEADS = 8
ATTN_WIDTH = N_HEADS * HEAD_DIM
MOBA_BLOCK = 256
MOBA_TOPK = 3
PAGE_SIZE = 128
PAGES_PER_BLOCK = MOBA_BLOCK // PAGE_SIZE
RMS_EPS = 1e-6
ATTN_SCALE = HEAD_DIM ** -0.5
LOG2E = 1.4426950408889634
NEG = -1e30
LANE = 128
SUBLANE = 8
VMEM_LIMIT = 56 * 1024 * 1024


def _alibi_slopes_np():
    return (2.0 ** (-8.0 * np.arange(1, N_HEADS + 1, dtype=np.float64) / N_HEADS)).astype(np.float32)


def _rms(x, g):
    y = x * lax.rsqrt(jnp.mean(x * x, axis=-1, keepdims=True) + RMS_EPS)
    return y * g


def _split_bf16(x):
    hi = x.astype(BF16)
    lo = (x - hi.astype(F32)).astype(BF16)
    return hi, lo


_DN_T = (((1,), (1,)), ((), ()))


MIX_CHUNK = 256


def _mix_project(x_ref, g_ref, w_ref, z_ref, finish=()):
    h = _rms(x_ref[...], g_ref[...]).astype(BF16)
    finish = iter(finish)
    for c0 in range(0, w_ref.shape[1], MIX_CHUNK):
        z_ref[:, c0:c0 + MIX_CHUNK] = jnp.dot(h, w_ref[:, c0:c0 + MIX_CHUNK],
                                              preferred_element_type=F32)
        next(finish, None)
    for _ in finish:
        pass


def _mix_finish(s, z_ref, wc_ref, gco_ref,
                k_ref, v_ref, kb_ref, vt_ref, qt_ref, st_ref, cn_ref, cst_ref, carry_ref, kmean_ref):
    tm = z_ref.shape[0]
    a = ATTN_WIDTH
    c = cn_ref.shape[1]
    first = s == 0

    k = z_ref[:, a:2 * a]
    v = z_ref[:, 2 * a:3 * a]
    k_ref[...] = k
    v_ref[...] = v
    kb_ref[...] = k.astype(BF16)
    vt = jnp.transpose(v)
    vt_ref[...] = vt.astype(BF16)
    ksum = jnp.sum(k, axis=0, keepdims=True) * (1.0 / MOBA_BLOCK)
    yield

    qt = jnp.transpose(z_ref[:, 0:a])
    qt_ref[...] = (qt * (ATTN_SCALE * LOG2E)).astype(BF16)
    yield

    gb = z_ref[:, 3 * a:3 * a + c]
    gc = z_ref[:, 3 * a + c:3 * a + 2 * c]
    u = z_ref[:, 3 * a + 2 * c:3 * a + 3 * c]
    xc = gc * u
    row = lax.broadcasted_iota(jnp.int32, xc.shape, 0)
    p0 = jnp.where(first, 0.0, carry_ref[0:1, :])
    p1 = jnp.where(first, 0.0, carry_ref[1:2, :])
    xc1 = jnp.where(row == 0, p1, pltpu.roll(xc, 1, axis=0))
    xc2 = jnp.where(row == 0, p0, jnp.where(row == 1, p1, pltpu.roll(xc, 2, axis=0)))
    y = wc_ref[0:1, :] * xc2 + wc_ref[1:2, :] * xc1 + wc_ref[2:3, :] * xc
    cn = _rms(gb * y, gco_ref[...])
    cn_ref[...] = cn.astype(BF16)
    last2 = xc[tm - 2:tm, :]
    carry_ref[...] = last2
    cst_ref[...] = last2
    yield

    nb = kmean_ref.shape[0]
    hmask = (lax.broadcasted_iota(jnp.int32, (N_HEADS, a), 1) // HEAD_DIM
             == lax.broadcasted_iota(jnp.int32, (N_HEADS, a), 0))
    kmt = jnp.where(hmask[None], kmean_ref[...][:, None, :], 0.0).reshape(nb * N_HEADS, a)
    qh, ql = _split_bf16(qt)
    kh, kl = _split_bf16(kmt)
    gate = (jnp.dot(kh, qh, preferred_element_type=F32) + jnp.dot(kh, ql, preferred_element_type=F32)
            + jnp.dot(kl, qh, preferred_element_type=F32))
    yield

    gm = [jnp.where(j < s, gate[j * N_HEADS:(j + 1) * N_HEADS, :], -jnp.inf) for j in range(nb)]
    cnt = [jnp.zeros((N_HEADS, tm), F32) for _ in range(nb)]
    for j in range(nb):
        for m in range(j):
            m_beats_j = gm[m] >= gm[j]
            cnt[j] = cnt[j] + jnp.where(m_beats_j, 1.0, 0.0)
            cnt[m] = cnt[m] + jnp.where(m_beats_j, 0.0, 1.0)
        if j == nb // 2:
            yield
    sel = [jnp.where(j < s, jnp.where(cnt[j] < MOBA_TOPK, 0.0, NEG), jnp.where(j == s, 0.0, NEG))
           for j in range(nb)]
    st_ref[...] = jnp.concatenate([sel[j][h:h + 1, :] for h in range(N_HEADS) for j in range(nb)],
                                  axis=0).astype(BF16)

    kmean_ref[pl.ds(s, 1), :] = ksum


def _mix_prompt_kernel(x_ref, g_ref, w_ref, wc_ref, gco_ref,
                       k_ref, v_ref, kb_ref, vt_ref, qt_ref, st_ref, cn_ref, cst_ref,
                       carry_ref, kmean_ref, za_ref, zb_ref, *, tiles_per_seq):
    t = pl.program_id(0)
    s_prev = (t + tiles_per_seq - 1) % tiles_per_seq
    outs = (k_ref, v_ref, kb_ref, vt_ref, qt_ref, st_ref, cn_ref, cst_ref, carry_ref, kmean_ref)

    @pl.when(t == 0)
    def _():
        carry_ref[...] = jnp.zeros_like(carry_ref)
        kmean_ref[...] = jnp.zeros_like(kmean_ref)
        _mix_project(x_ref, g_ref, w_ref, za_ref)

    @pl.when(t % 2 == 1)
    def _():
        _mix_project(x_ref, g_ref, w_ref, zb_ref, _mix_finish(s_prev, za_ref, wc_ref, gco_ref, *outs))

    @pl.when((t % 2 == 0) & (t > 0))
    def _():
        _mix_project(x_ref, g_ref, w_ref, za_ref, _mix_finish(s_prev, zb_ref, wc_ref, gco_ref, *outs))


def _mix_prompt(x, g, w_bf, wc, gco):
    b, s, d = x.shape
    tm = MOBA_BLOCK
    ns = s // tm
    nt = b * ns
    a = ATTN_WIDTH
    c = wc.shape[1]
    assert s % tm == 0 and ns * N_HEADS == LANE and w_bf.shape[1] == 3 * a + 3 * c and nt % 2 == 0
    tin = lambda t: jnp.minimum(t, nt - 1)
    tout = lambda t: jnp.maximum(t - 1, 0)
    row_in = lambda n: pl.BlockSpec((None, tm, n), lambda t: (tin(t) // ns, tin(t) % ns, 0))
    row_spec = lambda n: pl.BlockSpec((None, tm, n), lambda t: (tout(t) // ns, tout(t) % ns, 0))
    col_spec = lambda n: pl.BlockSpec((None, n, tm), lambda t: (tout(t) // ns, 0, tout(t) % ns))
    full = lambda shp: pl.BlockSpec(shp, lambda t: (0,) * len(shp))
    return pl.pallas_call(
        functools.partial(_mix_prompt_kernel, tiles_per_seq=ns),
        grid=(nt + 1,),
        in_specs=[row_in(d), full((1, d)), full(w_bf.shape), full(wc.shape), full((1, c))],
        out_specs=[row_spec(a), row_spec(a), row_spec(a),
                   pl.BlockSpec((None, None, a, tm), lambda t: (tout(t) // ns, tout(t) % ns, 0, 0)),
                   col_spec(a), col_spec(LANE), row_spec(c),
                   pl.BlockSpec((None, 2, c), lambda t: (tout(t) // ns, 0, 0))],
        out_shape=[jax.ShapeDtypeStruct((b, s, a), F32),
                   jax.ShapeDtypeStruct((b, s, a), F32),
                   jax.ShapeDtypeStruct((b, s, a), BF16),
                   jax.ShapeDtypeStruct((b, ns, a, tm), BF16),
                   jax.ShapeDtypeStruct((b, a, s), BF16),
                   jax.ShapeDtypeStruct((b, LANE, s), BF16),
                   jax.ShapeDtypeStruct((b, s, c), BF16),
                   jax.ShapeDtypeStruct((b, 2, c), F32)],
        scratch_shapes=[pltpu.VMEM((2, c), F32), pltpu.VMEM((ns, a), F32),
                        pltpu.VMEM((tm, w_bf.shape[1]), F32), pltpu.VMEM((tm, w_bf.shape[1]), F32)],
        compiler_params=pltpu.CompilerParams(dimension_semantics=("arbitrary",),
                                             vmem_limit_bytes=VMEM_LIMIT),
        name="mix_prompt",
    )(x, g, w_bf, wc, gco)


AUG_ROWS = 16


QK_LEAD = 2


def _attn_prompt_kernel(cst_ref, qt_ref, st_ref, kb_ref, vt_ref, o_ref, w_ref, sc_ref, *state_refs):
    i = pl.program_id(1)
    blk = MOBA_BLOCK
    hd = HEAD_DIM

    zq = jnp.zeros((hd, blk), BF16)
    zpad = jnp.zeros((LANE - 2 * AUG_ROWS, blk), BF16)
    for h in range(N_HEADS):
        qh = qt_ref[h * hd:(h + 1) * hd, :]
        qpart = [qh, zq] if h % 2 == 0 else [zq, qh]
        w_ref[h] = jnp.concatenate(
            qpart + [st_ref[h * AUG_ROWS:(h + 1) * AUG_ROWS, :], cst_ref[h], zpad], axis=0)

    elane = lax.broadcasted_iota(jnp.int32, (blk, LANE), 1)
    erow = lax.broadcasted_iota(jnp.int32, (blk, LANE), 0).astype(F32)
    ebase = jnp.where((elane >= AUG_ROWS) & (elane < AUG_ROWS + 3), erow, 0.0)
    rk = lax.broadcasted_iota(jnp.int32, (blk, blk), 0)
    rq = lax.broadcasted_iota(jnp.int32, (blk, blk), 1)
    ones_rows = jnp.ones((2 * SUBLANE, blk), BF16)

    def key_aug(j):
        off = ((j - i) * blk).astype(F32)
        return jnp.where(elane == j, 1.0,
                         jnp.where((elane >= AUG_ROWS + 3) & (elane < AUG_ROWS + 6), off, ebase)).astype(BF16)

    def scores(kbj, e, h):
        rhs = jnp.concatenate([kbj[:, (h // 2) * LANE:(h // 2 + 1) * LANE], e], axis=1)
        return jnp.dot(rhs, w_ref[h], preferred_element_type=F32)

    def block(j, nxt, diagonal, rd, wr):
        vtj = vt_ref[j]
        if nxt is not None:
            kbn = kb_ref[nxt]
            en = key_aug(nxt)

        def score_ahead(h):
            if nxt is not None and h < N_HEADS:
                sc_ref[wr, h] = scores(kbn, en, h)

        for h in range(QK_LEAD):
            score_ahead(h)
        for h in range(N_HEADS):
            st = sc_ref[rd, h]
            score_ahead(h + QK_LEAD)
            if diagonal:
                st = st + jnp.where(rk <= rq, 0.0, NEG)
            acc_ref, m_ref, l_ref = state_refs[h], state_refs[N_HEADS + h], state_refs[2 * N_HEADS + h]
            m_old = m_ref[...]
            m_new = jnp.maximum(m_old, jnp.max(st, axis=0, keepdims=True))
            alpha = jnp.exp2(m_old - m_new)
            p = jnp.exp2(st - m_new)
            m_ref[...] = m_new
            pv = jnp.dot(jnp.concatenate([vtj[h * hd:(h + 1) * hd, :], ones_rows], axis=0),
                         p.astype(BF16), preferred_element_type=F32)
            l_ref[...] = alpha * l_ref[...] + pv[hd:hd + 1, :]
            acc_ref[...] = alpha * acc_ref[...] + pv[0:hd, :]

    kb0 = kb_ref[0]
    e0 = key_aug(0)
    for h in range(N_HEADS):
        sc_ref[0, h] = scores(kb0, e0, h)
        state_refs[h][...] = jnp.zeros((hd, blk), F32)
        state_refs[N_HEADS + h][...] = jnp.full((1, blk), -jnp.inf, F32)
        state_refs[2 * N_HEADS + h][...] = jnp.zeros((1, blk), F32)

    def pair(t, carry):
        block(2 * t, 2 * t + 1, False, 0, 1)
        block(2 * t + 1, 2 * t + 2, False, 1, 0)
        return carry

    lax.fori_loop(0, i // 2, pair, 0)

    def finish(rd):
        block(i, None, True, rd, None)
        o_t = jnp.concatenate(
            [state_refs[h][...] / state_refs[2 * N_HEADS + h][...] for h in range(N_HEADS)], axis=0)
        o_ref[...] = jnp.transpose(o_t)

    @pl.when(i % 2 == 0)
    def _():
        finish(0)

    @pl.when(i % 2 == 1)
    def _():
        block(i - 1, i, False, 0, 1)
        finish(1)


def _attn_prompt(qt, st, kb, vt, cst):
    b, nb, blk, a = kb.shape
    assert blk == MOBA_BLOCK and nb <= AUG_ROWS and a == ATTN_WIDTH and st.shape[1] == N_HEADS * AUG_ROWS
    return pl.pallas_call(
        _attn_prompt_kernel,
        grid=(b, nb),
        in_specs=[pl.BlockSpec(cst.shape, lambda bi, i: (0, 0, 0)),
                  pl.BlockSpec((None, a, blk), lambda bi, i: (bi, 0, i)),
                  pl.BlockSpec((None, N_HEADS * AUG_ROWS, blk), lambda bi, i: (bi, 0, i)),
                  pl.BlockSpec((None, nb, blk, a), lambda bi, i: (bi, 0, 0, 0)),
                  pl.BlockSpec((None, nb, a, blk), lambda bi, i: (bi, 0, 0, 0))],
        out_specs=pl.BlockSpec((None, blk, a), lambda bi, i: (bi, i, 0)),
        out_shape=jax.ShapeDtypeStruct((b, nb * blk, a), F32),
        scratch_shapes=([pltpu.VMEM((N_HEADS, 2 * LANE, blk), BF16),
                         pltpu.VMEM((2, N_HEADS, blk, blk), F32)]
                        + [pltpu.VMEM((HEAD_DIM, blk), F32)] * N_HEADS
                        + [pltpu.VMEM((1, blk), F32)] * (2 * N_HEADS)),
        compiler_params=pltpu.CompilerParams(
            dimension_semantics=("arbitrary", "arbitrary"), vmem_limit_bytes=VMEM_LIMIT),
        name="attn_prompt",
    )(cst, qt, st, kb, vt)


class _NoSide:
    def region(self, n_tiles):
        pass

    def piece(self):
        return None

    def close(self):
        pass


MXU_TILE = 2 * LANE
TOKEN_SHAPE = (2 * SUBLANE, LANE)


def _zero_from(token):
    bits = pltpu.bitcast(token, jnp.uint32)
    half = jnp.uint32(16)
    return pltpu.bitcast(lax.shift_right_logical(lax.shift_right_logical(bits, half), half), F32)


def _after(x, token):
    if token is None:
        return x
    zero = _zero_from(token).astype(x.dtype)
    tr, tc = token.shape
    m, k = x.shape
    assert m % 2 == 0 and k % 2 == 0 and m // 2 >= tr and k // 2 >= tc

    def patch_cols(rows):
        return jnp.concatenate([rows[:, 0:tc] + zero, rows[:, tc:k // 2],
                                rows[:, k // 2:k // 2 + tc] + zero, rows[:, k // 2 + tc:]], axis=1)

    return jnp.concatenate([patch_cols(x[0:tr, :]), x[tr:m // 2, :],
                            patch_cols(x[m // 2:m // 2 + tr, :]), x[m // 2 + tr:, :]], axis=0)


def _ffn_body(x_ref, a_ref, cn_ref, gao_ref, wo_ref, gffn_ref, wg_ref, wu_ref, wd_ref, gfin_ref,
              y_ref, side):
    ff = wg_ref.shape[1]
    split = (ff // 2 + MXU_TILE - 1) // MXU_TILE * MXU_TILE

    def mm(lhs, w_ref, rows, cols):
        starts = list(range(cols.start, cols.stop, MXU_TILE))
        side.region(len(starts))
        tiles, token = [], None
        for c0 in starts:
            c1 = min(c0 + MXU_TILE, cols.stop)
            tiles.append(jnp.dot(_after(lhs, token), w_ref[rows, c0:c1], preferred_element_type=F32))
            token = side.piece()
        return jnp.concatenate(tiles, axis=1)

    d = x_ref.shape[1]
    an = _rms(a_ref[...], gao_ref[...]).astype(BF16)
    mixed = jnp.concatenate([an, cn_ref[...]], axis=1)
    x1 = x_ref[...] + mm(mixed, wo_ref, slice(None), range(0, d))
    h2 = _rms(x1, gffn_ref[...]).astype(BF16)
    x2 = x1
    for lo, hi in ((0, split), (split, ff)):
        gate = mm(h2, wg_ref, slice(None), range(lo, hi))
        up = mm(h2, wu_ref, slice(None), range(lo, hi))
        act = (gate * (1.0 / (1.0 + jnp.exp(-gate))) * up).astype(BF16)
        x2 = x2 + mm(act, wd_ref, slice(lo, hi), range(0, d))
    side.close()
    y_ref[...] = _rms(x2, gfin_ref[...])


N_FFN_DOTS = 7
N_RING = 64


def _split_even(items, parts):
    q, r = divmod(len(items), parts)
    sizes = [q + (1 if p < r else 0) for p in range(parts)]
    out, i = [], 0
    for n in sizes:
        if n:
            out.append(items[i:i + n])
        i += n
    return out


class _GateStream:
    def __init__(self, b, pt_ref, q_ref, ck_ref, idx_ref, buf_ref, sem_ref, part_ref):
        self.b, self.pt_ref, self.q_ref, self.ck_ref = b, pt_ref, q_ref, ck_ref
        self.idx_ref, self.buf_ref, self.sem_ref, self.part_ref = idx_ref, buf_ref, sem_ref, part_ref
        self.n_seq, self.n_pages = pt_ref.shape
        self.slices = _split_even(list(range(self.n_pages // PAGES_PER_BLOCK)), N_FFN_DOTS)
        self.k = -1
        self.pending = []
        self.token = None

        @pl.when(b == 0)
        def _():
            for page in range(N_RING):
                self._copy(0, page).start()

    def _copy(self, seq, page):
        return pltpu.make_async_copy(self.ck_ref.at[0, self.pt_ref[seq, page]],
                                     self.buf_ref.at[page % N_RING], self.sem_ref.at[page % N_RING])

    def _pages(self):
        blocks = self.slices[self.k]
        return range(blocks[0] * PAGES_PER_BLOCK, (blocks[-1] + 1) * PAGES_PER_BLOCK)

    def _finish_slice(self):
        while self.pending:
            self.piece()
        if 0 <= self.k < len(self.slices):
            self._refill(self._pages())

    def region(self, n_tiles):
        self._finish_slice()
        self.k += 1
        self.token = None
        if self.k < len(self.slices):
            for page in self._pages():
                self._copy(self.b, page).wait()
            self.pending = _split_even(self.slices[self.k], max(1, n_tiles - 1))

    def piece(self):
        if not self.pending:
            return None
        blocks = self.pending.pop(0)
        last = []
        for h in range(N_HEADS):
            qh = self.q_ref[h]
            if self.token is not None:
                tr = TOKEN_SHAPE[0]
                qh = jnp.concatenate([qh[0:tr] + _zero_from(self.token), qh[tr:]], axis=0)
            for n in blocks:
                ksum = None
                for half in range(PAGES_PER_BLOCK):
                    page = self.buf_ref[(n * PAGES_PER_BLOCK + half) % N_RING, h]
                    ksum = page if ksum is None else ksum + page
                prod = ksum * qh
                red = jnp.sum(prod.reshape(HEAD_DIM // SUBLANE, SUBLANE, PAGE_SIZE), axis=0)
                for shift in (SUBLANE // 2, SUBLANE // 4, SUBLANE // 8):
                    red = red + pltpu.roll(red, shift, axis=0)
                self.part_ref[h, n:n + 1, :] = red[0:1, :]
                last = (last + [red])[-2:]
        self.token = jnp.concatenate(last, axis=0)
        return self.token

    def _refill(self, pages):
        same = [p + N_RING for p in pages if p + N_RING < self.n_pages]
        nxt = [p + N_RING - self.n_pages for p in pages if p + N_RING >= self.n_pages]
        for page in same:
            self._copy(self.b, page).start()
        if nxt:

            @pl.when(self.b + 1 < self.n_seq)
            def _():
                for page in nxt:
                    self._copy(self.b + 1, page).start()

    def close(self):
        while self.k < len(self.slices) - 1:
            self.region(1)
        self._finish_slice()
        self.k = len(self.slices)
        nblk = self.n_pages // PAGES_PER_BLOCK
        g = jnp.sum(self.part_ref[...], axis=2)
        nidx = lax.broadcasted_iota(jnp.int32, g.shape, 1)
        picks = []
        for _ in range(MOBA_TOPK):
            mx = jnp.max(g, axis=1, keepdims=True)
            pick = jnp.min(jnp.where(g == mx, nidx, nblk), axis=1, keepdims=True)
            picks.append(pick)
            g = jnp.where(nidx == pick, -jnp.inf, g)
        self.idx_ref[...] = jnp.concatenate(picks, axis=1)


def _ffn_kernel(*refs, with_gate):
    if with_gate:
        pt_ref, q_ref, ck_ref = refs[0], refs[11], refs[12]
        ffn_refs = refs[1:11] + refs[13:14]
        side = _GateStream(pl.program_id(0), pt_ref, q_ref, ck_ref, *refs[14:18])
    else:
        ffn_refs = refs
        side = _NoSide()
    _ffn_body(*ffn_refs, side)


def _ffn(x, attn, cn, gao, wo, gffn, wg, wu, wd, gfin, tm, gate_inputs=None):
    n, d = x.shape
    a = attn.shape[1]
    c = cn.shape[1]
    assert n % tm == 0
    row = lambda w: pl.BlockSpec((tm, w), lambda i, *_: (i, 0))
    res = lambda arr: pl.BlockSpec(arr.shape, lambda i, *_: (0, 0), pipeline_mode=pl.Buffered(1))
    in_specs = [row(d), row(a), row(c), res(gao), res(wo), res(gffn), res(wg), res(wu), res(wd),
                res(gfin)]
    out_specs = [row(d)]
    out_shape = [jax.ShapeDtypeStruct((n, d), F32)]
    scratch = []
    args = (x, attn, cn, gao, wo, gffn, wg, wu, wd, gfin)
    if gate_inputs is not None:
        page_table, q_bc, cache_kt = gate_inputs
        n_seq, n_pages = page_table.shape
        nblk = n_pages // PAGES_PER_BLOCK
        assert n // tm == n_seq and n_pages % PAGES_PER_BLOCK == 0 and nblk >= MOBA_TOPK
        assert n_pages % N_RING == 0 and 2 * PAGES_PER_BLOCK * -(-nblk // N_FFN_DOTS) <= N_RING
        assert cache_kt.shape[2:] == (N_HEADS, HEAD_DIM, PAGE_SIZE) and HEAD_DIM % SUBLANE == 0
        in_specs += [pl.BlockSpec((None, N_HEADS, HEAD_DIM, PAGE_SIZE), lambda i, *_: (i, 0, 0, 0)),
                     pl.BlockSpec(memory_space=pl.ANY)]
        out_specs += [pl.BlockSpec((None, N_HEADS, MOBA_TOPK), lambda i, *_: (i, 0, 0))]
        out_shape += [jax.ShapeDtypeStruct((n_seq, N_HEADS, MOBA_TOPK), jnp.int32)]
        scratch = [pltpu.VMEM((N_RING, N_HEADS, HEAD_DIM, PAGE_SIZE), F32),
                   pltpu.SemaphoreType.DMA((N_RING,)),
                   pltpu.VMEM((N_HEADS, nblk, PAGE_SIZE), F32)]
        args = (page_table,) + args + (q_bc, cache_kt)
    grid_spec = pltpu.PrefetchScalarGridSpec(
        num_scalar_prefetch=0 if gate_inputs is None else 1,
        grid=(n // tm,), in_specs=in_specs, out_specs=out_specs, scratch_shapes=scratch)
    out = pl.pallas_call(
        functools.partial(_ffn_kernel, with_gate=gate_inputs is not None),
        grid_spec=grid_spec,
        out_shape=out_shape,
        compiler_params=pltpu.CompilerParams(dimension_semantics=("arbitrary",),
                                             vmem_limit_bytes=VMEM_LIMIT),
        name="ffn_%d" % n,
    )(*args)
    return out[0] if gate_inputs is None else out


def _mix_sample_kernel(x_ref, g_ref, w_ref, wq_ref, wc_ref, gco_ref, h0_ref, h1_ref,
                       q_ref, k_ref, v_ref, cn_ref, xc_ref):
    a = ATTN_WIDTH
    c = cn_ref.shape[1]
    hf = _rms(x_ref[...], g_ref[...])
    h = hf.astype(BF16)
    z = jnp.dot(h, w_ref[...], preferred_element_type=F32)
    hh, hl = _split_bf16(hf)
    wh, wl = _split_bf16(wq_ref[...])
    q_ref[...] = (jnp.dot(hh, wh, preferred_element_type=F32) + jnp.dot(hl, wh, preferred_element_type=F32)
                  + jnp.dot(hh, wl, preferred_element_type=F32))
    k_ref[...] = z[:, a:2 * a]
    v_ref[...] = z[:, 2 * a:3 * a]
    gb = z[:, 3 * a:3 * a + c]
    xc = z[:, 3 * a + c:3 * a + 2 * c] * z[:, 3 * a + 2 * c:3 * a + 3 * c]
    y = wc_ref[0:1, :] * h0_ref[...] + wc_ref[1:2, :] * h1_ref[...] + wc_ref[2:3, :] * xc
    cn_ref[...] = _rms(gb * y, gco_ref[...]).astype(BF16)
    xc_ref[...] = xc


def _mix_sample(x, g, w_bf, wq, wc, gco, h0, h1):
    n = x.shape[0]
    a = ATTN_WIDTH
    c = wc.shape[1]
    return pl.pallas_call(
        _mix_sample_kernel,
        out_shape=[jax.ShapeDtypeStruct((n, a), F32)] * 3
        + [jax.ShapeDtypeStruct((n, c), BF16), jax.ShapeDtypeStruct((n, c), F32)],
        compiler_params=pltpu.CompilerParams(vmem_limit_bytes=VMEM_LIMIT),
        name="mix_sample",
    )(x, g, w_bf, wq, wc, gco, h0, h1)


def _attn_sample_kernel(pt_ref, idx_ref, q_ref, kn_ref, vn_ref, ck_ref, cv_ref, o_ref,
                        kbuf_ref, vbuf_ref, sem_ref, *, past_len, slopes):
    b = pl.program_id(0)
    n_seq = pl.num_programs(0)
    n_slab = MOBA_TOPK * PAGES_PER_BLOCK
    slot = b % 2

    def copies(seq, sl):
        cps = []
        for h in range(N_HEADS):
            for t in range(MOBA_TOPK):
                blk = idx_ref[seq, h, t]
                for half in range(PAGES_PER_BLOCK):
                    pg = pt_ref[seq, blk * PAGES_PER_BLOCK + half]
                    u = t * PAGES_PER_BLOCK + half
                    cps.append(pltpu.make_async_copy(ck_ref.at[0, pg, h], kbuf_ref.at[sl, h, u],
                                                     sem_ref.at[0, sl]))
                    cps.append(pltpu.make_async_copy(cv_ref.at[0, pg, h], vbuf_ref.at[sl, h, u],
                                                     sem_ref.at[1, sl]))
        return cps

    @pl.when(b == 0)
    def _():
        for cp in copies(0, 0):
            cp.start()

    @pl.when(b + 1 < n_seq)
    def _():
        for cp in copies(b + 1, 1 - slot):
            cp.start()

    for cp in copies(b, slot):
        cp.wait()

    pos = lax.broadcasted_iota(jnp.int32, (1, PAGE_SIZE), 1)
    for h in range(N_HEADS):
        qh = q_ref[h] * ATTN_SCALE
        own = jnp.sum(qh * kn_ref[h], axis=0, keepdims=True)
        rows = []
        for t in range(MOBA_TOPK):
            blk = idx_ref[b, h, t]
            for half in range(PAGES_PER_BLOCK):
                u = t * PAGES_PER_BLOCK + half
                kpos = blk * MOBA_BLOCK + half * PAGE_SIZE + pos
                dist = (past_len - kpos).astype(F32)
                rows.append(jnp.sum(kbuf_ref[slot, h, u] * qh, axis=0, keepdims=True)
                            - float(slopes[h]) * dist)
        mrow = functools.reduce(jnp.maximum, rows)
        m = jnp.maximum(jnp.max(mrow, axis=1, keepdims=True), own)
        p_own = jnp.exp(own - m)
        den = p_own
        acc = jnp.zeros((HEAD_DIM, PAGE_SIZE), F32)
        for u in range(n_slab):
            p = jnp.exp(rows[u] - m)
            den = den + jnp.sum(p, axis=1, keepdims=True)
            acc = acc + vbuf_ref[slot, h, u] * p
        num = jnp.sum(acc, axis=1, keepdims=True) + (p_own * vn_ref[h])[:, 0:1]
        o_ref[:, h:h + 1] = num / den[:, 0:1]


def _attn_sample(page_table, idx, q_bc, kn_bc, vn_bc, cache_kt, cache_vt, past_len):
    n_seq = q_bc.shape[0]
    blk4 = pl.BlockSpec((None, N_HEADS, HEAD_DIM, PAGE_SIZE), lambda i, pt, ix: (i, 0, 0, 0))
    n_slab = MOBA_TOPK * PAGES_PER_BLOCK
    grid_spec = pltpu.PrefetchScalarGridSpec(
        num_scalar_prefetch=2,
        grid=(n_seq,),
        in_specs=[blk4, blk4, blk4, pl.BlockSpec(memory_space=pl.ANY), pl.BlockSpec(memory_space=pl.ANY)],
        out_specs=pl.BlockSpec((None, HEAD_DIM, N_HEADS), lambda i, pt, ix: (i, 0, 0)),
        scratch_shapes=[pltpu.VMEM((2, N_HEADS, n_slab, HEAD_DIM, PAGE_SIZE), F32),
                        pltpu.VMEM((2, N_HEADS, n_slab, HEAD_DIM, PAGE_SIZE), F32),
                        pltpu.SemaphoreType.DMA((2, 2))])
    kern = functools.partial(_attn_sample_kernel, past_len=past_len, slopes=_alibi_slopes_np())
    return pl.pallas_call(
        kern,
        grid_spec=grid_spec,
        out_shape=jax.ShapeDtypeStruct((n_seq, HEAD_DIM, N_HEADS), F32),
        compiler_params=pltpu.CompilerParams(dimension_semantics=("arbitrary",),
                                             vmem_limit_bytes=VMEM_LIMIT),
        name="attn_sample",
    )(page_table, idx, q_bc, kn_bc, vn_bc, cache_kt, cache_vt)


def kernel(x_prompt, x_sample, cache_k, cache_v, state_conv, page_table, norm_mix, w_mix, w_conv,
           norm_attn_out, norm_conv_out, w_o, norm_ffn, w_gate, w_up, w_down, norm_final):
    depth = w_mix.shape[0]
    assert depth == 1 and x_sample.shape[1] == 1
    b, s, d = x_prompt.shape
    n_seq = x_sample.shape[0]
    n_pages = page_table.shape[1]
    past_len = n_pages * PAGE_SIZE
    a = ATTN_WIDTH
    c = w_conv.shape[2]

    g_mix = norm_mix[0][None, :]
    g_ao = norm_attn_out[0][None, :]
    g_co = norm_conv_out[0][None, :]
    g_ffn = norm_ffn[0][None, :]
    g_fin = norm_final[None, :]
    w_mix_bf = w_mix[0].astype(BF16)
    w_o_bf = w_o[0].astype(BF16)
    w_gate_bf = w_gate[0].astype(BF16)
    w_up_bf = w_up[0].astype(BF16)
    w_down_bf = w_down[0].astype(BF16)
    wc = w_conv[0]

    sl = (_alibi_slopes_np().astype(np.float64) * LOG2E).astype(np.float32)
    hi = sl.astype(jnp.bfloat16).astype(np.float32)
    mid = (sl - hi).astype(jnp.bfloat16).astype(np.float32)
    lo = (sl - hi - mid).astype(jnp.bfloat16).astype(np.float32)
    cst_np = np.zeros((N_HEADS, AUG_ROWS, MOBA_BLOCK), np.float32)
    for r, piece in enumerate((hi, mid, lo, hi, mid, lo)):
        cst_np[:, r, :] = piece[:, None]
    cst = jnp.asarray(cst_np, dtype=BF16)

    xs = x_sample[:, 0, :]
    h0 = state_conv[0, :, 0, :]
    h1 = state_conv[0, :, 1, :]
    q_s, k_s, v_s, cn_s, xc_s = _mix_sample(xs, g_mix, w_mix_bf, w_mix[0][:, :a], wc, g_co, h0, h1)
    hs = (N_HEADS, HEAD_DIM)
    lane_bc = lambda t: jnp.broadcast_to(t.reshape((n_seq,) + hs + (1,)), (n_seq,) + hs + (PAGE_SIZE,))
    q_bc, kn_bc, vn_bc = lane_bc(q_s), lane_bc(k_s), lane_bc(v_s)
    cache_kt = jnp.transpose(cache_k, (0, 1, 3, 4, 2))
    cache_vt = jnp.transpose(cache_v, (0, 1, 3, 4, 2))

    k_p, v_p, kb_p, vt_p, qt_p, st_p, cn_p, cst_p = _mix_prompt(x_prompt, g_mix, w_mix_bf, wc, g_co)
    attn_p = _attn_prompt(qt_p, st_p, kb_p.reshape(b, s // MOBA_BLOCK, MOBA_BLOCK, a), vt_p, cst)
    y_p, idx = _ffn(x_prompt.reshape(b * s, d), attn_p.reshape(b * s, a), cn_p.reshape(b * s, c),
                    g_ao, w_o_bf, g_ffn, w_gate_bf, w_up_bf, w_down_bf, g_fin, tm=b * s // n_seq,
                    gate_inputs=(page_table, q_bc, cache_kt))
    y_p = y_p.reshape(b, s, d)

    attn_s = _attn_sample(page_table, idx, q_bc, kn_bc, vn_bc, cache_kt, cache_vt, past_len)
    attn_s = jnp.transpose(attn_s, (0, 2, 1)).reshape(n_seq, a)
    y_s = _ffn(xs, attn_s, cn_s, g_ao, w_o_bf, g_ffn, w_gate_bf, w_up_bf, w_down_bf, g_fin, tm=n_seq)

    return (y_p, y_s[:, None, :],
            k_p.reshape((1, b, s) + hs), v_p.reshape((1, b, s) + hs), cst_p[None],
            k_s.reshape((1, n_seq, 1) + hs), v_s.reshape((1, n_seq, 1) + hs),
            jnp.stack([h1, xc_s], axis=1)[None])
```

```python
import functools

import numpy as np
import jax
import jax.numpy as jnp
from jax import lax
from jax.experimental import pallas as pl
from jax.experimental.pallas import tpu as pltpu

F32 = jnp.float32
BF16 = jnp.bfloat16

HEAD_DIM = 64
N_HEADS = 8
ATTN_WIDTH = N_HEADS * HEAD_DIM
MOBA_BLOCK = 256
MOBA_TOPK = 3
PAGE_SIZE = 128
PAGES_PER_BLOCK = MOBA_BLOCK // PAGE_SIZE
RMS_EPS = 1e-6
ATTN_SCALE = HEAD_DIM ** -0.5
LOG2E = 1.4426950408889634
NEG = -1e30
LANE = 128
SUBLANE = 8
VMEM_LIMIT = 56 * 1024 * 1024

def _alibi_slopes_np():
    return (2.0 ** (-8.0 * np.arange(1, N_HEADS + 1, dtype=np.float64) / N_HEADS)).astype(np.float32)


def _rms(x, g):
    y = x * lax.rsqrt(jnp.mean(x * x, axis=-1, keepdims=True) + RMS_EPS)
    return y * g


def _split_bf16(x):
    hi = x.astype(BF16)
    lo = (x - hi.astype(F32)).astype(BF16)
    return hi, lo


_DN_T = (((1,), (1,)), ((), ()))


MIX_CHUNK = 256


def _mix_project(x_ref, g_ref, w_ref, z_ref, finish=()):
    h = _rms(x_ref[...], g_ref[...]).astype(BF16)
    finish = iter(finish)
    for c0 in range(0, w_ref.shape[1], MIX_CHUNK):
        z_ref[:, c0:c0 + MIX_CHUNK] = jnp.dot(h, w_ref[:, c0:c0 + MIX_CHUNK],
                                              preferred_element_type=F32)
        next(finish, None)
    for _ in finish:
        pass


def _mix_finish(s, z_ref, wc_ref, gco_ref,
                k_ref, v_ref, kb_ref, vt_ref, qt_ref, st_ref, cn_ref, cst_ref, carry_ref, kmean_ref):
    tm = z_ref.shape[0]
    a = ATTN_WIDTH
    c = cn_ref.shape[1]
    first = s == 0

    k = z_ref[:, a:2 * a]
    v = z_ref[:, 2 * a:3 * a]
    k_ref[...] = k
    v_ref[...] = v
    kb_ref[...] = k.astype(BF16)
    vt = jnp.transpose(v)
    vt_ref[...] = vt.astype(BF16)
    ksum = jnp.sum(k, axis=0, keepdims=True) * (1.0 / MOBA_BLOCK)
    yield

    qt = jnp.transpose(z_ref[:, 0:a])
    qt_ref[...] = (qt * (ATTN_SCALE * LOG2E)).astype(BF16)
    yield

    gb = z_ref[:, 3 * a:3 * a + c]
    gc = z_ref[:, 3 * a + c:3 * a + 2 * c]
    u = z_ref[:, 3 * a + 2 * c:3 * a + 3 * c]
    xc = gc * u
    row = lax.broadcasted_iota(jnp.int32, xc.shape, 0)
    p0 = jnp.where(first, 0.0, carry_ref[0:1, :])
    p1 = jnp.where(first, 0.0, carry_ref[1:2, :])
    xc1 = jnp.where(row == 0, p1, pltpu.roll(xc, 1, axis=0))
    xc2 = jnp.where(row == 0, p0, jnp.where(row == 1, p1, pltpu.roll(xc, 2, axis=0)))
    y = wc_ref[0:1, :] * xc2 + wc_ref[1:2, :] * xc1 + wc_ref[2:3, :] * xc
    cn = _rms(gb * y, gco_ref[...])
    cn_ref[...] = cn.astype(BF16)
    last2 = xc[tm - 2:tm, :]
    carry_ref[...] = last2
    cst_ref[...] = last2
    yield

    nb = kmean_ref.shape[0]
    hmask = (lax.broadcasted_iota(jnp.int32, (N_HEADS, a), 1) // HEAD_DIM
             == lax.broadcasted_iota(jnp.int32, (N_HEADS, a), 0))
    kmt = jnp.where(hmask[None], kmean_ref[...][:, None, :], 0.0).reshape(nb * N_HEADS, a)
    qh, ql = _split_bf16(qt)
    kh, kl = _split_bf16(kmt)
    gate = (jnp.dot(kh, qh, preferred_element_type=F32) + jnp.dot(kh, ql, preferred_element_type=F32)
            + jnp.dot(kl, qh, preferred_element_type=F32))
    yield

    gm = [jnp.where(j < s, gate[j * N_HEADS:(j + 1) * N_HEADS, :], -jnp.inf) for j in range(nb)]
    cnt = [jnp.zeros((N_HEADS, tm), F32) for _ in range(nb)]
    for j in range(nb):
        for m in range(j):
            m_beats_j = gm[m] >= gm[j]
            cnt[j] = cnt[j] + jnp.where(m_beats_j, 1.0, 0.0)
            cnt[m] = cnt[m] + jnp.where(m_beats_j, 0.0, 1.0)
        if j == nb // 2:
            yield
    sel = [jnp.where(j < s, jnp.where(cnt[j] < MOBA_TOPK, 0.0, NEG), jnp.where(j == s, 0.0, NEG))
           for j in range(nb)]
    st_ref[...] = jnp.concatenate([sel[j][h:h + 1, :] for h in range(N_HEADS) for j in range(nb)],
                                  axis=0).astype(BF16)

    kmean_ref[pl.ds(s, 1), :] = ksum


def _mix_prompt_kernel(x_ref, g_ref, w_ref, wc_ref, gco_ref,
                       k_ref, v_ref, kb_ref, vt_ref, qt_ref, st_ref, cn_ref, cst_ref,
                       carry_ref, kmean_ref, za_ref, zb_ref, *, tiles_per_seq):
    t = pl.program_id(0)
    s_prev = (t + tiles_per_seq - 1) % tiles_per_seq
    outs = (k_ref, v_ref, kb_ref, vt_ref, qt_ref, st_ref, cn_ref, cst_ref, carry_ref, kmean_ref)

    @pl.when(t == 0)
    def _():
        carry_ref[...] = jnp.zeros_like(carry_ref)
        kmean_ref[...] = jnp.zeros_like(kmean_ref)
        _mix_project(x_ref, g_ref, w_ref, za_ref)

    @pl.when(t % 2 == 1)
    def _():
        _mix_project(x_ref, g_ref, w_ref, zb_ref, _mix_finish(s_prev, za_ref, wc_ref, gco_ref, *outs))

    @pl.when((t % 2 == 0) & (t > 0))
    def _():
        _mix_project(x_ref, g_ref, w_ref, za_ref, _mix_finish(s_prev, zb_ref, wc_ref, gco_ref, *outs))


def _mix_prompt(x, g, w_bf, wc, gco):
    b, s, d = x.shape
    tm = MOBA_BLOCK
    ns = s // tm
    nt = b * ns
    a = ATTN_WIDTH
    c = wc.shape[1]
    assert s % tm == 0 and ns * N_HEADS == LANE and w_bf.shape[1] == 3 * a + 3 * c and nt % 2 == 0
    tin = lambda t: jnp.minimum(t, nt - 1)
    tout = lambda t: jnp.maximum(t - 1, 0)
    row_in = lambda n: pl.BlockSpec((None, tm, n), lambda t: (tin(t) // ns, tin(t) % ns, 0))
    row_spec = lambda n: pl.BlockSpec((None, tm, n), lambda t: (tout(t) // ns, tout(t) % ns, 0))
    col_spec = lambda n: pl.BlockSpec((None, n, tm), lambda t: (tout(t) // ns, 0, tout(t) % ns))
    full = lambda shp: pl.BlockSpec(shp, lambda t: (0,) * len(shp))
    return pl.pallas_call(
        functools.partial(_mix_prompt_kernel, tiles_per_seq=ns),
        grid=(nt + 1,),
        in_specs=[row_in(d), full((1, d)), full(w_bf.shape), full(wc.shape), full((1, c))],
        out_specs=[row_spec(a), row_spec(a), row_spec(a),
                   pl.BlockSpec((None, None, a, tm), lambda t: (tout(t) // ns, tout(t) % ns, 0, 0)),
                   col_spec(a), col_spec(LANE), row_spec(c),
                   pl.BlockSpec((None, 2, c), lambda t: (tout(t) // ns, 0, 0))],
        out_shape=[jax.ShapeDtypeStruct((b, s, a), F32),
                   jax.ShapeDtypeStruct((b, s, a), F32),
                   jax.ShapeDtypeStruct((b, s, a), BF16),
                   jax.ShapeDtypeStruct((b, ns, a, tm), BF16),
                   jax.ShapeDtypeStruct((b, a, s), BF16),
                   jax.ShapeDtypeStruct((b, LANE, s), BF16),
                   jax.ShapeDtypeStruct((b, s, c), BF16),
                   jax.ShapeDtypeStruct((b, 2, c), F32)],
        scratch_shapes=[pltpu.VMEM((2, c), F32), pltpu.VMEM((ns, a), F32),
                        pltpu.VMEM((tm, w_bf.shape[1]), F32), pltpu.VMEM((tm, w_bf.shape[1]), F32)],
        compiler_params=pltpu.CompilerParams(dimension_semantics=("arbitrary",),
                                             vmem_limit_bytes=VMEM_LIMIT),
        name="mix_prompt",
    )(x, g, w_bf, wc, gco)


AUG_ROWS = 16


QK_LEAD = 2


def _attn_prompt_kernel(cst_ref, qt_ref, st_ref, kb_ref, vt_ref, o_ref, w_ref, sc_ref, *state_refs):
    i = pl.program_id(1)
    blk = MOBA_BLOCK
    hd = HEAD_DIM

    zq = jnp.zeros((hd, blk), BF16)
    zpad = jnp.zeros((LANE - 2 * AUG_ROWS, blk), BF16)
    for h in range(N_HEADS):
        qh = qt_ref[h * hd:(h + 1) * hd, :]
        qpart = [qh, zq] if h % 2 == 0 else [zq, qh]
        w_ref[h] = jnp.concatenate(
            qpart + [st_ref[h * AUG_ROWS:(h + 1) * AUG_ROWS, :], cst_ref[h], zpad], axis=0)

    elane = lax.broadcasted_iota(jnp.int32, (blk, LANE), 1)
    erow = lax.broadcasted_iota(jnp.int32, (blk, LANE), 0).astype(F32)
    ebase = jnp.where((elane >= AUG_ROWS) & (elane < AUG_ROWS + 3), erow, 0.0)
    rk = lax.broadcasted_iota(jnp.int32, (blk, blk), 0)
    rq = lax.broadcasted_iota(jnp.int32, (blk, blk), 1)
    ones_rows = jnp.ones((2 * SUBLANE, blk), BF16)

    def key_aug(j):
        off = ((j - i) * blk).astype(F32)
        return jnp.where(elane == j, 1.0,
                         jnp.where((elane >= AUG_ROWS + 3) & (elane < AUG_ROWS + 6), off, ebase)).astype(BF16)

    def scores(kbj, e, h):
        rhs = jnp.concatenate([kbj[:, (h // 2) * LANE:(h // 2 + 1) * LANE], e], axis=1)
        return jnp.dot(rhs, w_ref[h], preferred_element_type=F32)

    def block(j, nxt, diagonal, rd, wr):
        vtj = vt_ref[j]
        if nxt is not None:
            kbn = kb_ref[nxt]
            en = key_aug(nxt)

        def score_ahead(h):
            if nxt is not None and h < N_HEADS:
                sc_ref[wr, h] = scores(kbn, en, h)

        for h in range(QK_LEAD):
            score_ahead(h)
        for h in range(N_HEADS):
            st = sc_ref[rd, h]
            score_ahead(h + QK_LEAD)
            if diagonal:
                st = st + jnp.where(rk <= rq, 0.0, NEG)
            acc_ref, m_ref, l_ref = state_refs[h], state_refs[N_HEADS + h], state_refs[2 * N_HEADS + h]
            m_old = m_ref[...]
            m_new = jnp.maximum(m_old, jnp.max(st, axis=0, keepdims=True))
            alpha = jnp.exp2(m_old - m_new)
            p = jnp.exp2(st - m_new)
            m_ref[...] = m_new
            pv = jnp.dot(jnp.concatenate([vtj[h * hd:(h + 1) * hd, :], ones_rows], axis=0),
                         p.astype(BF16), preferred_element_type=F32)
            l_ref[...] = alpha * l_ref[...] + pv[hd:hd + 1, :]
            acc_ref[...] = alpha * acc_ref[...] + pv[0:hd, :]

    kb0 = kb_ref[0]
    e0 = key_aug(0)
    for h in range(N_HEADS):
        sc_ref[0, h] = scores(kb0, e0, h)
        state_refs[h][...] = jnp.zeros((hd, blk), F32)
        state_refs[N_HEADS + h][...] = jnp.full((1, blk), -jnp.inf, F32)
        state_refs[2 * N_HEADS + h][...] = jnp.zeros((1, blk), F32)

    def pair(t, carry):
        block(2 * t, 2 * t + 1, False, 0, 1)
        block(2 * t + 1, 2 * t + 2, False, 1, 0)
        return carry

    lax.fori_loop(0, i // 2, pair, 0)

    def finish(rd):
        block(i, None, True, rd, None)
        o_t = jnp.concatenate(
            [state_refs[h][...] / state_refs[2 * N_HEADS + h][...] for h in range(N_HEADS)], axis=0)
        o_ref[...] = jnp.transpose(o_t)

    @pl.when(i % 2 == 0)
    def _():
        finish(0)

    @pl.when(i % 2 == 1)
    def _():
        block(i - 1, i, False, 0, 1)
        finish(1)


def _attn_prompt(qt, st, kb, vt, cst):
    b, nb, blk, a = kb.shape
    assert blk == MOBA_BLOCK and nb <= AUG_ROWS and a == ATTN_WIDTH and st.shape[1] == N_HEADS * AUG_ROWS
    return pl.pallas_call(
        _attn_prompt_kernel,
        grid=(b, nb),
        in_specs=[pl.BlockSpec(cst.shape, lambda bi, i: (0, 0, 0)),
                  pl.BlockSpec((None, a, blk), lambda bi, i: (bi, 0, i)),
                  pl.BlockSpec((None, N_HEADS * AUG_ROWS, blk), lambda bi, i: (bi, 0, i)),
                  pl.BlockSpec((None, nb, blk, a), lambda bi, i: (bi, 0, 0, 0)),
                  pl.BlockSpec((None, nb, a, blk), lambda bi, i: (bi, 0, 0, 0))],
        out_specs=pl.BlockSpec((None, blk, a), lambda bi, i: (bi, i, 0)),
        out_shape=jax.ShapeDtypeStruct((b, nb * blk, a), F32),
        scratch_shapes=([pltpu.VMEM((N_HEADS, 2 * LANE, blk), BF16),
                         pltpu.VMEM((2, N_HEADS, blk, blk), F32)]
                        + [pltpu.VMEM((HEAD_DIM, blk), F32)] * N_HEADS
                        + [pltpu.VMEM((1, blk), F32)] * (2 * N_HEADS)),
        compiler_params=pltpu.CompilerParams(
            dimension_semantics=("arbitrary", "arbitrary"), vmem_limit_bytes=VMEM_LIMIT),
        name="attn_prompt",
    )(cst, qt, st, kb, vt)


class _NoSide:
    def region(self, n_tiles):
        pass

    def piece(self):
        return None

    def close(self):
        pass


MXU_TILE = 2 * LANE
TOKEN_SHAPE = (2 * SUBLANE, LANE)


def _zero_from(token):
    bits = pltpu.bitcast(token, jnp.uint32)
    half = jnp.uint32(16)
    return pltpu.bitcast(lax.shift_right_logical(lax.shift_right_logical(bits, half), half), F32)


def _after(x, token):
    if token is None:
        return x
    zero = _zero_from(token).astype(x.dtype)
    tr, tc = token.shape
    m, k = x.shape
    assert m % 2 == 0 and k % 2 == 0 and m // 2 >= tr and k // 2 >= tc

    def patch_cols(rows):
        return jnp.concatenate([rows[:, 0:tc] + zero, rows[:, tc:k // 2],
                                rows[:, k // 2:k // 2 + tc] + zero, rows[:, k // 2 + tc:]], axis=1)

    return jnp.concatenate([patch_cols(x[0:tr, :]), x[tr:m // 2, :],
                            patch_cols(x[m // 2:m // 2 + tr, :]), x[m // 2 + tr:, :]], axis=0)


def _ffn_body(x_ref, a_ref, cn_ref, gao_ref, wo_ref, gffn_ref, wg_ref, wu_ref, wd_ref, gfin_ref,
              y_ref, side):
    ff = wg_ref.shape[1]
    split = (ff // 2 + MXU_TILE - 1) // MXU_TILE * MXU_TILE

    def mm(lhs, w_ref, rows, cols):
        starts = list(range(cols.start, cols.stop, MXU_TILE))
        side.region(len(starts))
        tiles, token = [], None
        for c0 in starts:
            c1 = min(c0 + MXU_TILE, cols.stop)
            tiles.append(jnp.dot(_after(lhs, token), w_ref[rows, c0:c1], preferred_element_type=F32))
            token = side.piece()
        return jnp.concatenate(tiles, axis=1)

    d = x_ref.shape[1]
    an = _rms(a_ref[...], gao_ref[...]).astype(BF16)
    mixed = jnp.concatenate([an, cn_ref[...]], axis=1)
    x1 = x_ref[...] + mm(mixed, wo_ref, slice(None), range(0, d))
    h2 = _rms(x1, gffn_ref[...]).astype(BF16)
    x2 = x1
    for lo, hi in ((0, split), (split, ff)):
        gate = mm(h2, wg_ref, slice(None), range(lo, hi))
        up = mm(h2, wu_ref, slice(None), range(lo, hi))
        act = (gate * (1.0 / (1.0 + jnp.exp(-gate))) * up).astype(BF16)
        x2 = x2 + mm(act, wd_ref, slice(lo, hi), range(0, d))
    side.close()
    y_ref[...] = _rms(x2, gfin_ref[...])


N_FFN_DOTS = 7
N_RING = 64


def _split_even(items, parts):
    q, r = divmod(len(items), parts)
    sizes = [q + (1 if p < r else 0) for p in range(parts)]
    out, i = [], 0
    for n in sizes:
        if n:
            out.append(items[i:i + n])
        i += n
    return out


class _GateStream:
    def __init__(self, b, pt_ref, q_ref, ck_ref, idx_ref, buf_ref, sem_ref, part_ref):
        self.b, self.pt_ref, self.q_ref, self.ck_ref = b, pt_ref, q_ref, ck_ref
        self.idx_ref, self.buf_ref, self.sem_ref, self.part_ref = idx_ref, buf_ref, sem_ref, part_ref
        self.n_seq, self.n_pages = pt_ref.shape
        self.slices = _split_even(list(range(self.n_pages // PAGES_PER_BLOCK)), N_FFN_DOTS)
        self.k = -1
        self.pending = []
        self.token = None

        @pl.when(b == 0)
        def _():
            for page in range(N_RING):
                self._copy(0, page).start()

    def _copy(self, seq, page):
        return pltpu.make_async_copy(self.ck_ref.at[0, self.pt_ref[seq, page]],
                                     self.buf_ref.at[page % N_RING], self.sem_ref.at[page % N_RING])

    def _pages(self):
        blocks = self.slices[self.k]
        return range(blocks[0] * PAGES_PER_BLOCK, (blocks[-1] + 1) * PAGES_PER_BLOCK)

    def _finish_slice(self):
        while self.pending:
            self.piece()
        if 0 <= self.k < len(self.slices):
            self._refill(self._pages())

    def region(self, n_tiles):
        self._finish_slice()
        self.k += 1
        self.token = None
        if self.k < len(self.slices):
            for page in self._pages():
                self._copy(self.b, page).wait()
            self.pending = _split_even(self.slices[self.k], max(1, n_tiles - 1))

    def piece(self):
        if not self.pending:
            return None
        blocks = self.pending.pop(0)
        last = []
        for h in range(N_HEADS):
            qh = self.q_ref[h]
            if self.token is not None:
                tr = TOKEN_SHAPE[0]
                qh = jnp.concatenate([qh[0:tr] + _zero_from(self.token), qh[tr:]], axis=0)
            for n in blocks:
                ksum = None
                for half in range(PAGES_PER_BLOCK):
                    page = self.buf_ref[(n * PAGES_PER_BLOCK + half) % N_RING, h]
                    ksum = page if ksum is None else ksum + page
                prod = ksum * qh
                red = jnp.sum(prod.reshape(HEAD_DIM // SUBLANE, SUBLANE, PAGE_SIZE), axis=0)
                for shift in (SUBLANE // 2, SUBLANE // 4, SUBLANE // 8):
                    red = red + pltpu.roll(red, shift, axis=0)
                self.part_ref[h, n:n + 1, :] = red[0:1, :]
                last = (last + [red])[-2:]
        self.token = jnp.concatenate(last, axis=0)
        return self.token

    def _refill(self, pages):
        same = [p + N_RING for p in pages if p + N_RING < self.n_pages]
        nxt = [p + N_RING - self.n_pages for p in pages if p + N_RING >= self.n_pages]
        for page in same:
            self._copy(self.b, page).start()
        if nxt:

            @pl.when(self.b + 1 < self.n_seq)
            def _():
                for page in nxt:
                    self._copy(self.b + 1, page).start()

    def close(self):
        while self.k < len(self.slices) - 1:
            self.region(1)
        self._finish_slice()
        self.k = len(self.slices)
        nblk = self.n_pages // PAGES_PER_BLOCK
        g = jnp.sum(self.part_ref[...], axis=2)
        nidx = lax.broadcasted_iota(jnp.int32, g.shape, 1)
        picks = []
        for _ in range(MOBA_TOPK):
            mx = jnp.max(g, axis=1, keepdims=True)
            pick = jnp.min(jnp.where(g == mx, nidx, nblk), axis=1, keepdims=True)
            picks.append(pick)
            g = jnp.where(nidx == pick, -jnp.inf, g)
        self.idx_ref[...] = jnp.concatenate(picks, axis=1)


def _ffn_kernel(*refs, with_gate):
    if with_gate:
        pt_ref, q_ref, ck_ref = refs[0], refs[11], refs[12]
        ffn_refs = refs[1:11] + refs[13:14]
        side = _GateStream(pl.program_id(0), pt_ref, q_ref, ck_ref, *refs[14:18])
    else:
        ffn_refs = refs
        side = _NoSide()
    _ffn_body(*ffn_refs, side)


def _ffn(x, attn, cn, gao, wo, gffn, wg, wu, wd, gfin, tm, gate_inputs=None):
    n, d = x.shape
    a = attn.shape[1]
    c = cn.shape[1]
    assert n % tm == 0
    row = lambda w: pl.BlockSpec((tm, w), lambda i, *_: (i, 0))
    res = lambda arr: pl.BlockSpec(arr.shape, lambda i, *_: (0, 0), pipeline_mode=pl.Buffered(1))
    in_specs = [row(d), row(a), row(c), res(gao), res(wo), res(gffn), res(wg), res(wu), res(wd),
                res(gfin)]
    out_specs = [row(d)]
    out_shape = [jax.ShapeDtypeStruct((n, d), F32)]
    scratch = []
    args = (x, attn, cn, gao, wo, gffn, wg, wu, wd, gfin)
    if gate_inputs is not None:
        page_table, q_bc, cache_kt = gate_inputs
        n_seq, n_pages = page_table.shape
        nblk = n_pages // PAGES_PER_BLOCK
        assert n // tm == n_seq and n_pages % PAGES_PER_BLOCK == 0 and nblk >= MOBA_TOPK
        assert n_pages % N_RING == 0 and 2 * PAGES_PER_BLOCK * -(-nblk // N_FFN_DOTS) <= N_RING
        assert cache_kt.shape[2:] == (N_HEADS, HEAD_DIM, PAGE_SIZE) and HEAD_DIM % SUBLANE == 0
        in_specs += [pl.BlockSpec((None, N_HEADS, HEAD_DIM, PAGE_SIZE), lambda i, *_: (i, 0, 0, 0)),
                     pl.BlockSpec(memory_space=pl.ANY)]
        out_specs += [pl.BlockSpec((None, N_HEADS, MOBA_TOPK), lambda i, *_: (i, 0, 0))]
        out_shape += [jax.ShapeDtypeStruct((n_seq, N_HEADS, MOBA_TOPK), jnp.int32)]
        scratch = [pltpu.VMEM((N_RING, N_HEADS, HEAD_DIM, PAGE_SIZE), F32),
                   pltpu.SemaphoreType.DMA((N_RING,)),
                   pltpu.VMEM((N_HEADS, nblk, PAGE_SIZE), F32)]
        args = (page_table,) + args + (q_bc, cache_kt)
    grid_spec = pltpu.PrefetchScalarGridSpec(
        num_scalar_prefetch=0 if gate_inputs is None else 1,
        grid=(n // tm,), in_specs=in_specs, out_specs=out_specs, scratch_shapes=scratch)
    out = pl.pallas_call(
        functools.partial(_ffn_kernel, with_gate=gate_inputs is not None),
        grid_spec=grid_spec,
        out_shape=out_shape,
        compiler_params=pltpu.CompilerParams(dimension_semantics=("arbitrary",),
                                             vmem_limit_bytes=VMEM_LIMIT),
        name="ffn_%d" % n,
    )(*args)
    return out[0] if gate_inputs is None else out


def _mix_sample_kernel(x_ref, g_ref, w_ref, wq_ref, wc_ref, gco_ref, h0_ref, h1_ref,
                       q_ref, k_ref, v_ref, cn_ref, xc_ref):
    a = ATTN_WIDTH
    c = cn_ref.shape[1]
    hf = _rms(x_ref[...], g_ref[...])
    h = hf.astype(BF16)
    z = jnp.dot(h, w_ref[...], preferred_element_type=F32)
    hh, hl = _split_bf16(hf)
    wh, wl = _split_bf16(wq_ref[...])
    q_ref[...] = (jnp.dot(hh, wh, preferred_element_type=F32) + jnp.dot(hl, wh, preferred_element_type=F32)
                  + jnp.dot(hh, wl, preferred_element_type=F32))
    k_ref[...] = z[:, a:2 * a]
    v_ref[...] = z[:, 2 * a:3 * a]
    gb = z[:, 3 * a:3 * a + c]
    xc = z[:, 3 * a + c:3 * a + 2 * c] * z[:, 3 * a + 2 * c:3 * a + 3 * c]
    y = wc_ref[0:1, :] * h0_ref[...] + wc_ref[1:2, :] * h1_ref[...] + wc_ref[2:3, :] * xc
    cn_ref[...] = _rms(gb * y, gco_ref[...]).astype(BF16)
    xc_ref[...] = xc


def _mix_sample(x, g, w_bf, wq, wc, gco, h0, h1):
    n = x.shape[0]
    a = ATTN_WIDTH
    c = wc.shape[1]
    return pl.pallas_call(
        _mix_sample_kernel,
        out_shape=[jax.ShapeDtypeStruct((n, a), F32)] * 3
        + [jax.ShapeDtypeStruct((n, c), BF16), jax.ShapeDtypeStruct((n, c), F32)],
        compiler_params=pltpu.CompilerParams(vmem_limit_bytes=VMEM_LIMIT),
        name="mix_sample",
    )(x, g, w_bf, wq, wc, gco, h0, h1)


def _attn_sample_kernel(pt_ref, idx_ref, q_ref, kn_ref, vn_ref, ck_ref, cv_ref, o_ref,
                        kbuf_ref, vbuf_ref, sem_ref, *, past_len, slopes):
    b = pl.program_id(0)
    n_seq = pl.num_programs(0)
    n_slab = MOBA_TOPK * PAGES_PER_BLOCK
    slot = b % 2

    def copies(seq, sl):
        cps = []
        for h in range(N_HEADS):
            for t in range(MOBA_TOPK):
                blk = idx_ref[seq, h, t]
                for half in range(PAGES_PER_BLOCK):
                    pg = pt_ref[seq, blk * PAGES_PER_BLOCK + half]
                    u = t * PAGES_PER_BLOCK + half
                    cps.append(pltpu.make_async_copy(ck_ref.at[0, pg, h], kbuf_ref.at[sl, h, u],
                                                     sem_ref.at[0, sl]))
                    cps.append(pltpu.make_async_copy(cv_ref.at[0, pg, h], vbuf_ref.at[sl, h, u],
                                                     sem_ref.at[1, sl]))
        return cps

    @pl.when(b == 0)
    def _():
        for n, cp in enumerate(copies(0, 0)):
            cp.start(priority=n % 2)

    @pl.when(b + 1 < n_seq)
    def _():
        for n, cp in enumerate(copies(b + 1, 1 - slot)):
            cp.start(priority=n % 2)

    for cp in copies(b, slot):
        cp.wait()

    pos = lax.broadcasted_iota(jnp.int32, (1, PAGE_SIZE), 1)
    for h in range(N_HEADS):
        qh = q_ref[h] * ATTN_SCALE
        own = jnp.sum(qh * kn_ref[h], axis=0, keepdims=True)
        rows = []
        for t in range(MOBA_TOPK):
            blk = idx_ref[b, h, t]
            for half in range(PAGES_PER_BLOCK):
                u = t * PAGES_PER_BLOCK + half
                kpos = blk * MOBA_BLOCK + half * PAGE_SIZE + pos
                dist = (past_len - kpos).astype(F32)
                rows.append(jnp.sum(kbuf_ref[slot, h, u] * qh, axis=0, keepdims=True)
                            - float(slopes[h]) * dist)
        mrow = functools.reduce(jnp.maximum, rows)
        m = jnp.maximum(jnp.max(mrow, axis=1, keepdims=True), own)
        p_own = jnp.exp(own - m)
        den = p_own
        acc = jnp.zeros((HEAD_DIM, PAGE_SIZE), F32)
        for u in range(n_slab):
            p = jnp.exp(rows[u] - m)
            den = den + jnp.sum(p, axis=1, keepdims=True)
            acc = acc + vbuf_ref[slot, h, u] * p
        num = jnp.sum(acc, axis=1, keepdims=True) + (p_own * vn_ref[h])[:, 0:1]
        o_ref[:, h:h + 1] = num / den[:, 0:1]


def _attn_sample(page_table, idx, q_bc, kn_bc, vn_bc, cache_kt, cache_vt, past_len):
    n_seq = q_bc.shape[0]
    blk4 = pl.BlockSpec((None, N_HEADS, HEAD_DIM, PAGE_SIZE), lambda i, pt, ix: (i, 0, 0, 0))
    n_slab = MOBA_TOPK * PAGES_PER_BLOCK
    grid_spec = pltpu.PrefetchScalarGridSpec(
        num_scalar_prefetch=2,
        grid=(n_seq,),
        in_specs=[blk4, blk4, blk4, pl.BlockSpec(memory_space=pl.ANY), pl.BlockSpec(memory_space=pl.ANY)],
        out_specs=pl.BlockSpec((None, HEAD_DIM, N_HEADS), lambda i, pt, ix: (i, 0, 0)),
        scratch_shapes=[pltpu.VMEM((2, N_HEADS, n_slab, HEAD_DIM, PAGE_SIZE), F32),
                        pltpu.VMEM((2, N_HEADS, n_slab, HEAD_DIM, PAGE_SIZE), F32),
                        pltpu.SemaphoreType.DMA((2, 2))])
    kern = functools.partial(_attn_sample_kernel, past_len=past_len, slopes=_alibi_slopes_np())
    return pl.pallas_call(
        kern,
        grid_spec=grid_spec,
        out_shape=jax.ShapeDtypeStruct((n_seq, HEAD_DIM, N_HEADS), F32),
        compiler_params=pltpu.CompilerParams(dimension_semantics=("arbitrary",),
                                             vmem_limit_bytes=VMEM_LIMIT),
        name="attn_sample",
    )(page_table, idx, q_bc, kn_bc, vn_bc, cache_kt, cache_vt)


def kernel(x_prompt, x_sample, cache_k, cache_v, state_conv, page_table, norm_mix, w_mix, w_conv,
           norm_attn_out, norm_conv_out, w_o, norm_ffn, w_gate, w_up, w_down, norm_final):
    depth = w_mix.shape[0]
    assert depth == 1 and x_sample.shape[1] == 1
    b, s, d = x_prompt.shape
    n_seq = x_sample.shape[0]
    n_pages = page_table.shape[1]
    past_len = n_pages * PAGE_SIZE
    a = ATTN_WIDTH
    c = w_conv.shape[2]

    g_mix = norm_mix[0][None, :]
    g_ao = norm_attn_out[0][None, :]
    g_co = norm_conv_out[0][None, :]
    g_ffn = norm_ffn[0][None, :]
    g_fin = norm_final[None, :]
    w_mix_bf = w_mix[0].astype(BF16)
    w_o_bf = w_o[0].astype(BF16)
    w_gate_bf = w_gate[0].astype(BF16)
    w_up_bf = w_up[0].astype(BF16)
    w_down_bf = w_down[0].astype(BF16)
    wc = w_conv[0]

    sl = (_alibi_slopes_np().astype(np.float64) * LOG2E).astype(np.float32)
    hi = sl.astype(jnp.bfloat16).astype(np.float32)
    mid = (sl - hi).astype(jnp.bfloat16).astype(np.float32)
    lo = (sl - hi - mid).astype(jnp.bfloat16).astype(np.float32)
    cst_np = np.zeros((N_HEADS, AUG_ROWS, MOBA_BLOCK), np.float32)
    for r, piece in enumerate((hi, mid, lo, hi, mid, lo)):
        cst_np[:, r, :] = piece[:, None]
    cst = jnp.asarray(cst_np, dtype=BF16)

    xs = x_sample[:, 0, :]
    h0 = state_conv[0, :, 0, :]
    h1 = state_conv[0, :, 1, :]
    q_s, k_s, v_s, cn_s, xc_s = _mix_sample(xs, g_mix, w_mix_bf, w_mix[0][:, :a], wc, g_co, h0, h1)
    hs = (N_HEADS, HEAD_DIM)
    lane_bc = lambda t: jnp.broadcast_to(t.reshape((n_seq,) + hs + (1,)), (n_seq,) + hs + (PAGE_SIZE,))
    q_bc, kn_bc, vn_bc = lane_bc(q_s), lane_bc(k_s), lane_bc(v_s)
    cache_kt = jnp.transpose(cache_k, (0, 1, 3, 4, 2))
    cache_vt = jnp.transpose(cache_v, (0, 1, 3, 4, 2))

    k_p, v_p, kb_p, vt_p, qt_p, st_p, cn_p, cst_p = _mix_prompt(x_prompt, g_mix, w_mix_bf, wc, g_co)
    attn_p = _attn_prompt(qt_p, st_p, kb_p.reshape(b, s // MOBA_BLOCK, MOBA_BLOCK, a), vt_p, cst)
    y_p, idx = _ffn(x_prompt.reshape(b * s, d), attn_p.reshape(b * s, a), cn_p.reshape(b * s, c),
                    g_ao, w_o_bf, g_ffn, w_gate_bf, w_up_bf, w_down_bf, g_fin, tm=b * s // n_seq,
                    gate_inputs=(page_table, q_bc, cache_kt))
    y_p = y_p.reshape(b, s, d)

    attn_s = _attn_sample(page_table, idx, q_bc, kn_bc, vn_bc, cache_kt, cache_vt, past_len)
    attn_s = jnp.transpose(attn_s, (0, 2, 1)).reshape(n_seq, a)
    y_s = _ffn(xs, attn_s, cn_s, g_ao, w_o_bf, g_ffn, w_gate_bf, w_up_bf, w_down_bf, g_fin, tm=n_seq)

    return (y_p, y_s[:, None, :],
            k_p.reshape((1, b, s) + hs), v_p.reshape((1, b, s) + hs), cst_p[None],
            k_s.reshape((1, n_seq, 1) + hs), v_s.reshape((1, n_seq, 1) + hs),
            jnp.stack([h1, xc_s], axis=1)[None])
```
